```python
import math
import jax, jax.numpy as jnp
from jax import lax
import numpy as np

D_MODEL = 1024
BATCH = 1
SEQ = 16384
DEPTH = 2
DEC_BATCH = 8
DEC_SEQ = 2048
PAST_LEN = 128

N_META = 16
D_HYENA = D_MODEL // 2
D_CONF = D_MODEL - D_HYENA
D_IN_EVEN = 3 * D_HYENA + 2 * D_CONF
SHORT_K = 3
CONF_K = 31
FILTER_EMB = 33
FILTER_HIDDEN = 64
N_HEADS = 16
QK_NOPE = 64
QK_ROPE = 32
V_HEAD = 64
Q_RANK = 384
KV_RANK = 256
ROPE_THETA = 10000.0
D_FF = 4 * D_MODEL
Q_BLOCK = 128
N_EVEN = (DEPTH + 1) // 2
N_ODD = DEPTH // 2
DN_ALPHA = (2 * DEPTH) ** 0.25
DN_BETA = (8 * DEPTH) ** -0.25
LN_EPS = 1e-5
RMS_EPS = 1e-6
DECAY_TARGET = 1e-2
FAST_DECAY_PCT = 0.3
SLOW_DECAY_PCT = 1.5

kernel_name = "hybrid_hyena_conformer_mla_encoder"


def layer_norm(x, g, b):
    xf = x.astype(jnp.float32)
    mu = jnp.mean(xf, axis=-1, keepdims=True)
    var = jnp.mean(jnp.square(xf - mu), axis=-1, keepdims=True)
    y = (xf - mu) * lax.rsqrt(var + LN_EPS) * g.astype(jnp.float32) + b.astype(jnp.float32)
    return y.astype(x.dtype)


def rms_norm(x, g):
    xf = x.astype(jnp.float32)
    y = xf * lax.rsqrt(jnp.mean(jnp.square(xf), axis=-1, keepdims=True) + RMS_EPS) * g.astype(jnp.float32)
    return y.astype(x.dtype)


def depthwise_conv(x, w, b):
    k = w.shape[0]
    pad = (k - 1) // 2
    y = lax.conv_general_dilated(
        x, w[:, None, :].astype(x.dtype), window_strides=(1,), padding=[(pad, pad)],
        dimension_numbers=("NWC", "WIO", "NWC"), feature_group_count=x.shape[-1])
    return y + b.astype(x.dtype)


def hyena_filters(L, w1, b1, f1, w2, b2, f2, w3, decay):
    f32 = jnp.float32
    t = jnp.arange(L, dtype=f32) / max(L - 1, 1)
    bands = (FILTER_EMB - 1) // 2
    freqs = jnp.linspace(1e-4, bands - 1, bands, dtype=f32)
    w = 2.0 * math.pi * jnp.arange(L, dtype=f32) / L
    ang = w[:, None] * freqs[None, :]
    z = jnp.concatenate([t[:, None], jnp.cos(ang), -jnp.sin(ang)], axis=-1)
    h = jnp.sin(f1.astype(f32)[:, None, :] * (jnp.einsum('le,def->dlf', z, w1.astype(f32)) + b1.astype(f32)[:, None, :]))
    h = jnp.sin(f2.astype(f32)[:, None, :] * (jnp.einsum('dlf,dfg->dlg', h, w2.astype(f32)) + b2.astype(f32)[:, None, :]))
    h = jnp.einsum('dlf,dfc->dlc', h, w3.astype(f32))
    h = h * jnp.exp(-t[None, :, None] * decay.astype(f32)[:, None, :])
    c = h.shape[-1]
    taps = jnp.concatenate([h[0], jnp.zeros((1, c), f32), h[1, 1:][::-1]], axis=0)
    return taps * lax.rsqrt(jnp.sum(jnp.square(taps), axis=0, keepdims=True))


def hyena_mix(u, taps, skip_d):
    x0, x1, v = jnp.split(u, 3, axis=-1)
    L = u.shape[1]
    zf = (x1 * v).astype(jnp.float32)
    zspec = jnp.fft.rfft(zf, n=2 * L, axis=1)
    hspec = jnp.fft.rfft(taps, n=2 * L, axis=0)
    y = jnp.fft.irfft(zspec * hspec[None], n=2 * L, axis=1)[:, :L]
    y = y + zf * skip_d.astype(jnp.float32)
    return (x0.astype(jnp.float32) * y).astype(u.dtype)


def conformer_conv(u, dw_w, dw_b, ln_g, ln_b):
    a, g = jnp.split(u, 2, axis=-1)
    h = a * jax.nn.sigmoid(g)
    h = depthwise_conv(h, dw_w, dw_b)
    h = layer_norm(h, ln_g, ln_b)
    return jax.nn.silu(h)


def even_mixer(x, w_in, b_in, short_w, short_b, f_w1, f_b1, f_fr1, f_w2, f_b2, f_fr2, f_w3,
               decay, skip_d, dw_w, dw_b, cln_g, cln_b, w_out, b_out):
    L = x.shape[1]
    proj = x @ w_in + b_in
    hy_in = depthwise_conv(proj[..., :3 * D_HYENA], short_w, short_b)
    taps = hyena_filters(L, f_w1, f_b1, f_fr1, f_w2, f_b2, f_fr2, f_w3, decay)
    y_a = hyena_mix(hy_in, taps, skip_d)
    y_b = conformer_conv(proj[..., 3 * D_HYENA:], dw_w, dw_b, cln_g, cln_b)
    return jnp.concatenate([y_a, y_b], axis=-1) @ w_out + b_out


def rope_tables(L):
    pos = jnp.arange(L, dtype=jnp.float32)
    inv = 1.0 / (ROPE_THETA ** (jnp.arange(0, QK_ROPE, 2, dtype=jnp.float32) / QK_ROPE))
    ang = pos[:, None] * inv[None, :]
    return jnp.cos(ang), jnp.sin(ang)


def apply_rope(x, cos, sin):
    x1, x2 = jnp.split(x, 2, axis=-1)
    cos = cos.astype(x.dtype)
    sin = sin.astype(x.dtype)
    return jnp.concatenate([x1 * cos - x2 * sin, x1 * sin + x2 * cos], axis=-1)


def mla_mixer(x, wq_a, q_norm, wq_b, wkv_a, kv_norm, wkv_b, wo):
    B, L, _ = x.shape
    cq = rms_norm(x @ wq_a, q_norm)
    q = (cq @ wq_b).reshape(B, L, N_HEADS, QK_NOPE + QK_ROPE)
    q_nope, q_pe = q[..., :QK_NOPE], q[..., QK_NOPE:]
    kv = x @ wkv_a
    ckv = rms_norm(kv[..., :KV_RANK], kv_norm)
    cos, sin = rope_tables(L)
    q_pe = apply_rope(q_pe, cos[None, :, None, :], sin[None, :, None, :])
    k_pe = apply_rope(kv[..., KV_RANK:], cos[None], sin[None])
    kvb = (ckv @ wkv_b).reshape(B, L, N_HEADS, QK_NOPE + V_HEAD)
    k_nope, v = kvb[..., :QK_NOPE], kvb[..., QK_NOPE:]
    n_blk = -(-L // Q_BLOCK)
    Lp = n_blk * Q_BLOCK
    pad = ((0, 0), (0, Lp - L), (0, 0), (0, 0))
    qn = jnp.pad(q_nope, pad).reshape(B, n_blk, Q_BLOCK, N_HEADS, QK_NOPE).transpose(1, 0, 2, 3, 4)
    qp = jnp.pad(q_pe, pad).reshape(B, n_blk, Q_BLOCK, N_HEADS, QK_ROPE).transpose(1, 0, 2, 3, 4)
    scale = (QK_NOPE + QK_ROPE) ** -0.5

    def attend(blk):
        qn_b, qp_b = blk
        s = jnp.einsum('bqhd,bkhd->bhqk', qn_b, k_nope) + jnp.einsum('bqhr,bkr->bhqk', qp_b, k_pe)
        p = jax.nn.softmax(s.astype(jnp.float32) * scale, axis=-1).astype(v.dtype)
        return jnp.einsum('bhqk,bkhd->bqhd', p, v)

    o = lax.map(attend, (qn, qp))
    o = o.transpose(1, 0, 2, 3, 4).reshape(B, Lp, N_HEADS * V_HEAD)[:, :L]
    return o @ wo


def trunk(x, p):
    B = x.shape[0]
    meta = jnp.broadcast_to(p['meta_tokens'][None].astype(x.dtype), (B, N_META, D_MODEL))
    h = jnp.concatenate([meta, x], axis=1)
    for layer in range(DEPTH):
        i = layer // 2
        if layer % 2 == 0:
            mix = even_mixer(h, p['ev_w_in'][i], p['ev_b_in'][i], p['ev_short_w'][i], p['ev_short_b'][i],
                             p['hy_w1'][i], p['hy_b1'][i], p['hy_freq1'][i], p['hy_w2'][i], p['hy_b2'][i],
                             p['hy_freq2'][i], p['hy_w3'][i], p['hy_decay'][i], p['hy_skip_d'][i],
                             p['cf_dw_w'][i], p['cf_dw_b'][i], p['cf_ln_g'][i], p['cf_ln_b'][i],
                             p['ev_w_out'][i], p['ev_b_out'][i])
        else:
            mix = mla_mixer(h, p['mla_wq_a'][i], p['mla_q_norm'][i], p['mla_wq_b'][i], p['mla_wkv_a'][i],
                            p['mla_kv_norm'][i], p['mla_wkv_b'][i], p['mla_wo'][i])
        h = layer_norm(DN_ALPHA * h + mix, p['ln1_g'][layer], p['ln1_b'][layer])
        f = jnp.square(jax.nn.relu(h @ p['mlp_w1'][layer])) @ p['mlp_w2'][layer]
        h = layer_norm(DN_ALPHA * h + f, p['ln2_g'][layer], p['ln2_b'][layer])
    return h[:, N_META:]


def setup_inputs(seed: int = 0) -> dict:
    key = jax.random.key(seed)
    ks = iter(jax.random.split(key, 48))

    def nrm(shape, scale=1.0):
        return jax.random.normal(next(ks), shape, jnp.float32) * scale

    def gain(shape):
        return 1.0 + nrm(shape, 0.01)

    fast = abs(math.log(DECAY_TARGET) / FAST_DECAY_PCT)
    slow = abs(math.log(DECAY_TARGET) / SLOW_DECAY_PCT)
    base_decay = jnp.linspace(slow, fast, D_HYENA, dtype=jnp.float32)
    d = {}
    d['x_prompt'] = nrm((BATCH, SEQ, D_MODEL))
    d['x_sample'] = nrm((DEC_BATCH, DEC_SEQ, D_MODEL))
    d['meta_tokens'] = nrm((N_META, D_MODEL))
    d['ev_w_in'] = nrm((N_EVEN, D_MODEL, D_IN_EVEN), D_MODEL ** -0.5)
    d['ev_b_in'] = nrm((N_EVEN, D_IN_EVEN), 0.01)
    d['ev_short_w'] = nrm((N_EVEN, SHORT_K, 3 * D_HYENA), SHORT_K ** -0.5)
    d['ev_short_b'] = nrm((N_EVEN, 3 * D_HYENA), 0.01)
    d['hy_w1'] = nrm((N_EVEN, 2, FILTER_EMB, FILTER_HIDDEN), FILTER_EMB ** -0.5)
    d['hy_b1'] = nrm((N_EVEN, 2, FILTER_HIDDEN), 0.01)
    d['hy_freq1'] = gain((N_EVEN, 2, FILTER_HIDDEN))
    d['hy_w2'] = nrm((N_EVEN, 2, FILTER_HIDDEN, FILTER_HIDDEN), FILTER_HIDDEN ** -0.5)
    d['hy_b2'] = nrm((N_EVEN, 2, FILTER_HIDDEN), 0.01)
    d['hy_freq2'] = gain((N_EVEN, 2, FILTER_HIDDEN))
    d['hy_w3'] = nrm((N_EVEN, 2, FILTER_HIDDEN, D_HYENA), FILTER_HIDDEN ** -0.5)
    d['hy_decay'] = base_decay * (1.0 + nrm((N_EVEN, 2, D_HYENA), 0.05))
    d['hy_skip_d'] = nrm((N_EVEN, D_HYENA))
    d['cf_dw_w'] = nrm((N_EVEN, CONF_K, D_CONF), CONF_K ** -0.5)
    d['cf_dw_b'] = nrm((N_EVEN, D_CONF), 0.01)
    d['cf_ln_g'] = gain((N_EVEN, D_CONF))
    d['cf_ln_b'] = nrm((N_EVEN, D_CONF), 0.01)
    d['ev_w_out'] = nrm((N_EVEN, D_HYENA + D_CONF, D_MODEL), DN_BETA * (D_HYENA + D_CONF) ** -0.5)
    d['ev_b_out'] = nrm((N_EVEN, D_MODEL), 0.01)
    d['mla_wq_a'] = nrm((N_ODD, D_MODEL, Q_RANK), D_MODEL ** -0.5)
    d['mla_q_norm'] = gain((N_ODD, Q_RANK))
    d['mla_wq_b'] = nrm((N_ODD, Q_RANK, N_HEADS * (QK_NOPE + QK_ROPE)), Q_RANK ** -0.5)
    d['mla_wkv_a'] = nrm((N_ODD, D_MODEL, KV_RANK + QK_ROPE), D_MODEL ** -0.5)
    d['mla_kv_norm'] = gain((N_ODD, KV_RANK))
    d['mla_wkv_b'] = nrm((N_ODD, KV_RANK, N_HEADS * (QK_NOPE + V_HEAD)), KV_RANK ** -0.5)
    d['mla_wo'] = nrm((N_ODD, N_HEADS * V_HEAD, D_MODEL), DN_BETA * (N_HEADS * V_HEAD) ** -0.5)
    d['ln1_g'] = gain((DEPTH, D_MODEL))
    d['ln1_b'] = nrm((DEPTH, D_MODEL), 0.01)
    d['mlp_w1'] = nrm((DEPTH, D_MODEL, D_FF), D_MODEL ** -0.5)
    d['mlp_w2'] = nrm((DEPTH, D_FF, D_MODEL), DN_BETA * D_FF ** -0.5)
    d['ln2_g'] = gain((DEPTH, D_MODEL))
    d['ln2_b'] = nrm((DEPTH, D_MODEL), 0.01)
    return d


def reference(x_prompt, x_sample, meta_tokens, ev_w_in, ev_b_in, ev_short_w, ev_short_b,
              hy_w1, hy_b1, hy_freq1, hy_w2, hy_b2, hy_freq2, hy_w3, hy_decay, hy_skip_d,
              cf_dw_w, cf_dw_b, cf_ln_g, cf_ln_b, ev_w_out, ev_b_out,
              mla_wq_a, mla_q_norm, mla_wq_b, mla_wkv_a, mla_kv_norm, mla_wkv_b, mla_wo,
              ln1_g, ln1_b, mlp_w1, mlp_w2, ln2_g, ln2_b):
    params = dict(meta_tokens=meta_tokens, ev_w_in=ev_w_in, ev_b_in=ev_b_in, ev_short_w=ev_short_w,
                  ev_short_b=ev_short_b, hy_w1=hy_w1, hy_b1=hy_b1, hy_freq1=hy_freq1, hy_w2=hy_w2,
                  hy_b2=hy_b2, hy_freq2=hy_freq2, hy_w3=hy_w3, hy_decay=hy_decay, hy_skip_d=hy_skip_d,
                  cf_dw_w=cf_dw_w, cf_dw_b=cf_dw_b, cf_ln_g=cf_ln_g, cf_ln_b=cf_ln_b,
                  ev_w_out=ev_w_out, ev_b_out=ev_b_out, mla_wq_a=mla_wq_a, mla_q_norm=mla_q_norm,
                  mla_wq_b=mla_wq_b, mla_wkv_a=mla_wkv_a, mla_kv_norm=mla_kv_norm, mla_wkv_b=mla_wkv_b,
                  mla_wo=mla_wo, ln1_g=ln1_g, ln1_b=ln1_b, mlp_w1=mlp_w1, mlp_w2=mlp_w2,
                  ln2_g=ln2_g, ln2_b=ln2_b)
    y_prompt = trunk(x_prompt, params)
    y_sample = trunk(x_sample, params)
    return (y_prompt, y_sample)
```

```python
import functools
import math

import jax
import jax.numpy as jnp
import numpy as np
from jax import lax
from jax.experimental import pallas as pl
from jax.experimental.pallas import tpu as pltpu

F32 = jnp.float32
BF16 = jnp.bfloat16

D_MODEL = 1024
N_META = 16
D_HYENA = 512
D_CONF = 512
D_IN_EVEN = 3 * D_HYENA + 2 * D_CONF
CONF_K = 31
FILTER_EMB = 33
FILTER_EMB_PAD = 40
FILTER_HIDDEN = 64
N_HEADS = 16
QK_NOPE = 64
QK_ROPE = 32
V_HEAD = 64
Q_RANK = 384
KV_RANK = 256
ROPE_THETA = 10000.0
D_FF = 4096
DEPTH = 2
DN_ALPHA = (2 * DEPTH) ** 0.25
LN_EPS = 1e-5
RMS_EPS = 1e-6

LANES = 128
SUBLANES = 8
HEAD_SLOT = LANES
HALO = 16
VMEM_LIMIT = 56 * 1024 * 1024

PAD_LANE = QK_NOPE + QK_ROPE
SUM_LANE = V_HEAD
PAD_SCORE = -1e30
FFT_N2 = 128

HIGHEST = lax.Precision.HIGHEST


def _params(sem, vmem=VMEM_LIMIT):
    return pltpu.CompilerParams(dimension_semantics=sem, vmem_limit_bytes=vmem)


def _round_up(x, m):
    return -(-x // m) * m


def _layer_norm(x, g, b):
    mu = jnp.mean(x, axis=-1, keepdims=True)
    xc = x - mu
    var = jnp.mean(xc * xc, axis=-1, keepdims=True)
    return xc * lax.rsqrt(var + LN_EPS) * g + b


def _mm_bias_kernel(x_ref, w_ref, b_ref, o_ref):
    acc = jnp.dot(x_ref[...].astype(BF16), w_ref[...], preferred_element_type=F32)
    o_ref[...] = (acc + b_ref[...]).astype(o_ref.dtype)


def _mm_bias(x, w, b, tm, out_dtype=F32):
    rows, k = x.shape
    n = w.shape[1]
    return pl.pallas_call(
        _mm_bias_kernel,
        grid=(pl.cdiv(rows, tm),),
        in_specs=[
            pl.BlockSpec((tm, k), lambda i: (i, 0)),
            pl.BlockSpec((k, n), lambda i: (0, 0)),
            pl.BlockSpec((1, n), lambda i: (0, 0)),
        ],
        out_specs=pl.BlockSpec((tm, n), lambda i: (i, 0)),
        out_shape=jax.ShapeDtypeStruct((rows, n), out_dtype),
        compiler_params=_params(("parallel",)),
        name="in_proj",
    )(x, w, b)


def _mm_res_ln_kernel(n_in, *refs):
    a_refs = refs[:n_in]
    w_refs = refs[n_in:2 * n_in]
    bias_ref, res_ref, g_ref, b_ref, o_ref = refs[2 * n_in:]
    acc = bias_ref[...] + DN_ALPHA * res_ref[...]
    for a_ref, w_ref in zip(a_refs, w_refs):
        acc = acc + jnp.dot(a_ref[...].astype(BF16), w_ref[...], preferred_element_type=F32)
    o_ref[...] = _layer_norm(acc, g_ref[...], b_ref[...])


def _mm_res_ln(a_list, w_list, bias, res, g, b, tm, name):
    rows = res.shape[0]
    n_in = len(a_list)
    in_specs = [pl.BlockSpec((tm, a.shape[1]), lambda i: (i, 0)) for a in a_list]
    in_specs += [pl.BlockSpec(w.shape, lambda i: (0, 0)) for w in w_list]
    in_specs += [
        pl.BlockSpec((1, D_MODEL), lambda i: (0, 0)),
        pl.BlockSpec((tm, D_MODEL), lambda i: (i, 0)),
        pl.BlockSpec((1, D_MODEL), lambda i: (0, 0)),
        pl.BlockSpec((1, D_MODEL), lambda i: (0, 0)),
    ]
    return pl.pallas_call(
        functools.partial(_mm_res_ln_kernel, n_in),
        grid=(pl.cdiv(rows, tm),),
        in_specs=in_specs,
        out_specs=pl.BlockSpec((tm, D_MODEL), lambda i: (i, 0)),
        out_shape=jax.ShapeDtypeStruct((rows, D_MODEL), F32),
        compiler_params=_params(("parallel",)),
        name=name,
    )(*a_list, *w_list, bias, res, g, b)


def _mlp_kernel(x_ref, w1_ref, w2_ref, g_ref, b_ref, o_ref, acc_ref):
    k = pl.program_id(1)

    @pl.when(k == 0)
    def _():
        acc_ref[...] = jnp.zeros_like(acc_ref)

    hid = jnp.dot(x_ref[...].astype(BF16), w1_ref[...], preferred_element_type=F32)
    hid = jnp.maximum(hid, 0.0)
    hid = (hid * hid).astype(BF16)
    acc_ref[...] += jnp.dot(hid, w2_ref[...], preferred_element_type=F32)

    @pl.when(k == pl.num_programs(1) - 1)
    def _():
        o_ref[...] = _layer_norm(DN_ALPHA * x_ref[...] + acc_ref[...], g_ref[...], b_ref[...])


def _mlp(x, w1, w2, g, b, tm, tk):
    rows = x.shape[0]
    return pl.pallas_call(
        _mlp_kernel,
        grid=(pl.cdiv(rows, tm), D_FF // tk),
        in_specs=[
            pl.BlockSpec((tm, D_MODEL), lambda i, k: (i, 0)),
            pl.BlockSpec((D_MODEL, tk), lambda i, k: (0, k)),
            pl.BlockSpec((tk, D_MODEL), lambda i, k: (k, 0)),
            pl.BlockSpec((1, D_MODEL), lambda i, k: (0, 0)),
            pl.BlockSpec((1, D_MODEL), lambda i, k: (0, 0)),
        ],
        out_specs=pl.BlockSpec((tm, D_MODEL), lambda i, k: (i, 0)),
        out_shape=jax.ShapeDtypeStruct((rows, D_MODEL), F32),
        scratch_shapes=[pltpu.VMEM((tm, D_MODEL), F32)],
        compiler_params=_params(("parallel", "arbitrary")),
        name="mlp",
    )(x, w1, w2, g, b)


def _conv_mix_kernel(l_true, tl, rc,
                     main_ref, prev_ref, next_ref, sw_ref, sb_ref, dw_ref, db_ref, g_ref, b_ref,
                     x0_ref, z_ref, yb_ref, ext_ref, glu_ref):
    i = pl.program_id(1)
    row0 = i * tl - HALO

    def valid_rows(start, n):
        g = start + lax.broadcasted_iota(jnp.int32, (n, 1), 0)
        return (g >= 0) & (g < l_true)

    ext_ref[0:HALO, :] = jnp.where(valid_rows(row0, HALO), prev_ref[0], 0.0)
    ext_ref[HALO:HALO + tl, :] = jnp.where(valid_rows(row0 + HALO, tl), main_ref[0], 0.0)
    ext_ref[HALO + tl:, :] = jnp.where(valid_rows(row0 + HALO + tl, HALO), next_ref[0], 0.0)

    a = ext_ref[:, 3 * D_HYENA:3 * D_HYENA + D_CONF]
    gate = ext_ref[:, 3 * D_HYENA + D_CONF:]
    glu_ref[...] = a * jax.nn.sigmoid(gate)

    sw = sw_ref[...]
    sb = sb_ref[...]
    dw = dw_ref[...]
    half = (CONF_K - 1) // 2

    def chunk(c, carry):
        r0 = pl.multiple_of(c * rc, SUBLANES)
        rows_ok = valid_rows(i * tl + r0, rc)

        win = ext_ref[pl.ds(r0 + HALO - SUBLANES, rc + 2 * SUBLANES), 0:3 * D_HYENA]
        hy = sb
        for k in range(3):
            off = SUBLANES + k - 1
            hy = hy + sw[k:k + 1, :] * win[off:off + rc, :]
        x0_ref[0, pl.ds(r0, rc), :] = hy[:, 0:D_HYENA]
        zz = hy[:, D_HYENA:2 * D_HYENA] * hy[:, 2 * D_HYENA:]
        z_ref[0, pl.ds(r0, rc), :] = jnp.where(rows_ok, zz, 0.0)

        gwin = glu_ref[pl.ds(r0, rc + 2 * HALO), :]
        acc = db_ref[...]
        for k in range(CONF_K):
            off = HALO + k - half
            acc = acc + dw[k:k + 1, :] * gwin[off:off + rc, :]
        y = _layer_norm(acc, g_ref[...], b_ref[...])
        yb_ref[0, pl.ds(r0, rc), :] = y * jax.nn.sigmoid(y)
        return carry

    lax.fori_loop(0, tl // rc, chunk, 0)


def _conv_mix(proj, l_true, tl, short_w, short_b, dw_w, dw_b, ln_g, ln_b):
    bsz, lp, _ = proj.shape
    nt = lp // tl
    hb = tl // HALO
    rc = _divisor_tile(tl, 64, SUBLANES)
    out_sds = jax.ShapeDtypeStruct((bsz, lp, D_HYENA), F32)
    row_spec = lambda shape: pl.BlockSpec(shape, lambda b, i: (0, 0))
    return pl.pallas_call(
        functools.partial(_conv_mix_kernel, l_true, tl, rc),
        grid=(bsz, nt),
        in_specs=[
            pl.BlockSpec((1, tl, D_IN_EVEN), lambda b, i: (b, i, 0)),
            pl.BlockSpec((1, HALO, D_IN_EVEN), lambda b, i: (b, jnp.maximum(i * hb - 1, 0), 0)),
            pl.BlockSpec((1, HALO, D_IN_EVEN), lambda b, i: (b, jnp.minimum((i + 1) * hb, nt * hb - 1), 0)),
            row_spec((3, 3 * D_HYENA)),
            row_spec((1, 3 * D_HYENA)),
            row_spec((CONF_K, D_CONF)),
            row_spec((1, D_CONF)),
            row_spec((1, D_CONF)),
            row_spec((1, D_CONF)),
        ],
        out_specs=[pl.BlockSpec((1, tl, D_HYENA), lambda b, i: (b, i, 0))] * 3,
        out_shape=[out_sds, out_sds, out_sds],
        scratch_shapes=[
            pltpu.VMEM((tl + 2 * HALO, D_IN_EVEN), F32),
            pltpu.VMEM((tl + 2 * HALO, D_CONF), F32),
        ],
        compiler_params=_params(("parallel", "parallel")),
        name="conv_mix",
    )(proj, proj, proj, short_w, short_b, dw_w, dw_b, ln_g, ln_b)


def _filter_kernel(l_true, tl, zf_ref, w1_ref, b1_ref, f1_ref, w2_ref, b2_ref, f2_ref, w3_ref, dec_ref,
                   h_ref, ss_ref):
    i = pl.program_id(0)

    @pl.when(i == 0)
    def _():
        ss_ref[...] = jnp.zeros_like(ss_ref)

    zf = zf_ref[...]
    t = zf[:, 0:1]
    g = i * tl + lax.broadcasted_iota(jnp.int32, (tl, 1), 0)
    total = jnp.zeros((1, D_HYENA), F32)
    for d in range(2):
        a = jnp.dot(zf, w1_ref[d], precision=HIGHEST, preferred_element_type=F32) + b1_ref[d]
        a = jnp.sin(f1_ref[d] * a)
        a = jnp.dot(a, w2_ref[d], precision=HIGHEST, preferred_element_type=F32) + b2_ref[d]
        a = jnp.sin(f2_ref[d] * a)
        hh = jnp.dot(a, w3_ref[d], precision=HIGHEST, preferred_element_type=F32)
        hh = hh * jnp.exp(-t * dec_ref[d])
        keep = (g < l_true) & (g >= d)
        hh = jnp.where(keep, hh, 0.0)
        h_ref[d] = hh
        total = total + jnp.sum(hh * hh, axis=0, keepdims=True)
    ss_ref[...] += total


def _hyena_filter(zf, l_true, tl, w1, b1, f1, w2, b2, f2, w3, decay):
    lp = zf.shape[0]
    full = lambda shape: pl.BlockSpec(shape, lambda i: (0,) * len(shape))
    return pl.pallas_call(
        functools.partial(_filter_kernel, l_true, tl),
        grid=(lp // tl,),
        in_specs=[
            pl.BlockSpec((tl, FILTER_EMB_PAD), lambda i: (i, 0)),
            full((2, FILTER_EMB_PAD, FILTER_HIDDEN)),
            full((2, 1, FILTER_HIDDEN)),
            full((2, 1, FILTER_HIDDEN)),
            full((2, FILTER_HIDDEN, FILTER_HIDDEN)),
            full((2, 1, FILTER_HIDDEN)),
            full((2, 1, FILTER_HIDDEN)),
            full((2, FILTER_HIDDEN, D_HYENA)),
            full((2, 1, D_HYENA)),
        ],
        out_specs=[
            pl.BlockSpec((2, tl, D_HYENA), lambda i: (0, i, 0)),
            pl.BlockSpec((1, D_HYENA), lambda i: (0, 0)),
        ],
        out_shape=[
            jax.ShapeDtypeStruct((2, lp, D_HYENA), F32),
            jax.ShapeDtypeStruct((1, D_HYENA), F32),
        ],
        compiler_params=_params(("arbitrary",)),
        name="hyena_filter",
    )(zf, w1, b1, f1, w2, b2, f2, w3, decay)


def _dft_a_kernel(kz, kzp, f_ref, x_ref, o_ref):
    x = x_ref[0]
    if kzp != kz:
        x = jnp.concatenate([x, jnp.zeros((kzp - kz, x.shape[1]), F32)], axis=0)
    o_ref[0] = jnp.dot(f_ref[...], x, precision=HIGHEST, preferred_element_type=F32)


def _dft_a(fmat, x, tn):
    bsz, kz, cols = x.shape
    m, kzp = fmat.shape
    return pl.pallas_call(
        functools.partial(_dft_a_kernel, kz, kzp),
        grid=(bsz, cols // tn),
        in_specs=[
            pl.BlockSpec((m, kzp), lambda b, j: (0, 0)),
            pl.BlockSpec((1, kz, tn), lambda b, j: (b, 0, j)),
        ],
        out_specs=pl.BlockSpec((1, m, tn), lambda b, j: (b, 0, j)),
        out_shape=jax.ShapeDtypeStruct((bsz, m, cols), F32),
        compiler_params=_params(("parallel", "parallel")),
        name="dft_stage_a",
    )(fmat, x)


def _dft_b_filter_kernel(m_ref, a_ref, rs_ref, kf_ref):
    mm = m_ref[0]
    fwd = jnp.concatenate([a_ref[0, 0, 0], a_ref[0, 1, 0]], axis=0)
    bwd = jnp.concatenate([a_ref[1, 0, 0], a_ref[1, 1, 0]], axis=0)
    zf = jnp.dot(mm, fwd, precision=HIGHEST, preferred_element_type=F32)
    zb = jnp.dot(mm, bwd, precision=HIGHEST, preferred_element_type=F32)
    rs = rs_ref[...]
    kf_ref[0, 0] = (zf[:FFT_N2] + zb[:FFT_N2]) * rs
    kf_ref[1, 0] = (zf[FFT_N2:] - zb[FFT_N2:]) * rs


def _dft_b_filter(mmat, a, rs):
    n1 = mmat.shape[0]
    a5 = a.reshape(2, 2, n1, FFT_N2, D_HYENA)
    return pl.pallas_call(
        _dft_b_filter_kernel,
        grid=(n1,),
        in_specs=[
            pl.BlockSpec((1, 2 * FFT_N2, 2 * FFT_N2), lambda f: (f, 0, 0)),
            pl.BlockSpec((2, 2, 1, FFT_N2, D_HYENA), lambda f: (0, 0, f, 0, 0)),
            pl.BlockSpec((1, D_HYENA), lambda f: (0, 0)),
        ],
        out_specs=pl.BlockSpec((2, 1, FFT_N2, D_HYENA), lambda f: (0, f, 0, 0)),
        out_shape=jax.ShapeDtypeStruct((2, n1, FFT_N2, D_HYENA), F32),
        compiler_params=_params(("parallel",)),
        name="dft_filter_spectrum",
    )(mmat, a5, rs)


def _dft_b_kernel(m_ref, mt_ref, a_ref, kf_ref, o_ref):
    x = jnp.concatenate([a_ref[0, 0, 0], a_ref[0, 1, 0]], axis=0)
    z = jnp.dot(m_ref[0], x, precision=HIGHEST, preferred_element_type=F32)
    zr, zi = z[:FFT_N2], z[FFT_N2:]
    kr, ki = kf_ref[0, 0], kf_ref[1, 0]
    y = jnp.concatenate([zr * kr - zi * ki, zr * ki + zi * kr], axis=0)
    w = jnp.dot(mt_ref[0], y, precision=HIGHEST, preferred_element_type=F32)
    o_ref[0, 0, 0] = w[:FFT_N2]
    o_ref[0, 1, 0] = w[FFT_N2:]


def _dft_b(mmat, mmat_t, a, kf):
    bsz = a.shape[0]
    n1 = mmat.shape[0]
    a5 = a.reshape(bsz, 2, n1, FFT_N2, D_HYENA)
    out = pl.pallas_call(
        _dft_b_kernel,
        grid=(bsz, n1),
        in_specs=[
            pl.BlockSpec((1, 2 * FFT_N2, 2 * FFT_N2), lambda b, f: (f, 0, 0)),
            pl.BlockSpec((1, 2 * FFT_N2, 2 * FFT_N2), lambda b, f: (f, 0, 0)),
            pl.BlockSpec((1, 2, 1, FFT_N2, D_HYENA), lambda b, f: (b, 0, f, 0, 0)),
            pl.BlockSpec((2, 1, FFT_N2, D_HYENA), lambda b, f: (0, f, 0, 0)),
        ],
        out_specs=pl.BlockSpec((1, 2, 1, FFT_N2, D_HYENA), lambda b, f: (b, 0, f, 0, 0)),
        out_shape=jax.ShapeDtypeStruct((bsz, 2, n1, FFT_N2, D_HYENA), F32),
        compiler_params=_params(("parallel", "parallel")),
        name="dft_stage_b",
    )(mmat, mmat_t, a5, kf)
    return out.reshape(bsz, 2 * n1, FFT_N2 * D_HYENA)


def _dft_c_kernel(kz, g_ref, w_ref, x0_ref, z_ref, d_ref, o_ref):
    y = jnp.dot(g_ref[...], w_ref[0], precision=HIGHEST, preferred_element_type=F32)
    zz = z_ref[0]
    o_ref[0] = x0_ref[0] * (y[:kz] + zz * d_ref[...])


def _dft_c(gmat, w, x0, z, d_tiled, tn):
    bsz, kz, cols = x0.shape
    kzp, m = gmat.shape
    return pl.pallas_call(
        functools.partial(_dft_c_kernel, kz),
        grid=(bsz, cols // tn),
        in_specs=[
            pl.BlockSpec((kzp, m), lambda b, j: (0, 0)),
            pl.BlockSpec((1, m, tn), lambda b, j: (b, 0, j)),
            pl.BlockSpec((1, kz, tn), lambda b, j: (b, 0, j)),
            pl.BlockSpec((1, kz, tn), lambda b, j: (b, 0, j)),
            pl.BlockSpec((1, tn), lambda b, j: (0, j)),
        ],
        out_specs=pl.BlockSpec((1, kz, tn), lambda b, j: (b, 0, j)),
        out_shape=jax.ShapeDtypeStruct((bsz, kz, cols), F32),
        compiler_params=_params(("parallel", "parallel")),
        name="dft_stage_c",
    )(gmat, w, x0, z, d_tiled)


def _dft_tables(n1, kz, kzp):
    n2 = FFT_N2
    n = n1 * n2
    f1 = jnp.arange(n1, dtype=jnp.int32)
    s1 = jnp.arange(kzp, dtype=jnp.int32)
    ang = (2.0 * math.pi / n1) * ((f1[:, None] * s1[None, :]) % n1).astype(F32)
    live = (s1 < kz)[None, :]
    fa = jnp.concatenate([jnp.where(live, jnp.cos(ang), 0.0), jnp.where(live, -jnp.sin(ang), 0.0)], axis=0)
    ga = jnp.concatenate([jnp.cos(ang).T, -jnp.sin(ang).T], axis=1) * jnp.where(s1 < kz, 1.0 / n, 0.0)[:, None]
    f2 = jnp.arange(n2, dtype=jnp.int32)
    s2 = jnp.arange(n2, dtype=jnp.int32)
    phase = (s2[None, None, :] * f1[:, None, None] + n1 * ((f2[None, :, None] * s2[None, None, :]) % n2)) % n
    ang2 = (2.0 * math.pi / n) * phase.astype(F32)
    cr, ci = jnp.cos(ang2), -jnp.sin(ang2)
    mmat = jnp.concatenate([jnp.concatenate([cr, -ci], axis=2), jnp.concatenate([ci, cr], axis=2)], axis=1)
    return fa, ga, mmat, jnp.swapaxes(mmat, 1, 2)


def _rms(x, g):
    return x * lax.rsqrt(jnp.mean(x * x, axis=-1, keepdims=True) + RMS_EPS) * g


def _mla_proj_kernel(l_true, tm,
                     x_ref, cos_ref, sin_ref, wqa_ref, qn_ref, wq1_ref, wq2_ref, wc_ref, kn_ref,
                     wk1_ref, wk2_ref, wkk_ref, wkv_ref, q_ref, k_ref, v_ref):
    i = pl.program_id(1)
    x = x_ref[0].astype(BF16)
    cos = cos_ref[...]
    sin = sin_ref[...]
    lane = lax.broadcasted_iota(jnp.int32, (1, HEAD_SLOT), 1)

    cq = _rms(jnp.dot(x, wqa_ref[...], preferred_element_type=F32), qn_ref[...]).astype(BF16)
    q1 = jnp.dot(cq, wq1_ref[...], preferred_element_type=F32)
    q2 = jnp.dot(cq, wq2_ref[...], preferred_element_type=F32)
    q_one = jnp.where(lane == PAD_LANE, 1.0, 0.0)

    ckv = _rms(jnp.dot(x, wc_ref[...], preferred_element_type=F32), kn_ref[...]).astype(BF16)
    kn = jnp.dot(ckv, wkk_ref[...], preferred_element_type=F32)
    vv = jnp.dot(ckv, wkv_ref[...], preferred_element_type=F32)
    k1 = jnp.dot(x, wk1_ref[...], preferred_element_type=F32)
    k2 = jnp.dot(x, wk2_ref[...], preferred_element_type=F32)
    rows = i * tm + lax.broadcasted_iota(jnp.int32, (tm, 1), 0)
    k_pe = k1 * cos + k2 * sin + jnp.where((rows >= l_true) & (lane == PAD_LANE), PAD_SCORE, 0.0)
    v_one = jnp.where(lane == SUM_LANE, 1.0, 0.0)

    for h in range(N_HEADS):
        sl = slice(h * HEAD_SLOT, (h + 1) * HEAD_SLOT)
        q_ref[0, :, sl] = (q1[:, sl] * cos + q2[:, sl] * sin + q_one).astype(BF16)
        k_ref[0, :, sl] = (kn[:, sl] + k_pe).astype(BF16)
        v_ref[0, :, sl] = (vv[:, sl] + v_one).astype(BF16)


def _mla_proj(h, l_true, tm, cos_t, sin_t, wqa, qn, wq1, wq2, wc, kn, wk1, wk2, wkk, wkv):
    bsz, lp, _ = h.shape
    wide = N_HEADS * HEAD_SLOT
    full = lambda a: pl.BlockSpec(a.shape, lambda b, i: (0,) * a.ndim)
    out_sds = jax.ShapeDtypeStruct((bsz, lp, wide), BF16)
    weights = (wqa, qn, wq1, wq2, wc, kn, wk1, wk2, wkk, wkv)
    return pl.pallas_call(
        functools.partial(_mla_proj_kernel, l_true, tm),
        grid=(bsz, lp // tm),
        in_specs=[
            pl.BlockSpec((1, tm, D_MODEL), lambda b, i: (b, i, 0)),
            pl.BlockSpec((tm, HEAD_SLOT), lambda b, i: (i, 0)),
            pl.BlockSpec((tm, HEAD_SLOT), lambda b, i: (i, 0)),
        ] + [full(w) for w in weights],
        out_specs=[pl.BlockSpec((1, tm, wide), lambda b, i: (b, i, 0))] * 3,
        out_shape=[out_sds, out_sds, out_sds],
        compiler_params=_params(("parallel", "parallel")),
        name="mla_proj",
    )(h, cos_t, sin_t, *weights)


def _attn_kernel(tk, q_ref, k_ref, v_ref, o_ref, m_ref, acc_ref):
    q = q_ref[0]
    nk = k_ref.shape[1] // tk
    m_ref[...] = jnp.full_like(m_ref, -jnp.inf)
    acc_ref[...] = jnp.zeros_like(acc_ref)

    def step(c, carry):
        start = pl.multiple_of(c * tk, tk)
        k = k_ref[0, pl.ds(start, tk), :]
        v = v_ref[0, pl.ds(start, tk), :]
        s = lax.dot_general(q, k, (((1,), (1,)), ((), ())), preferred_element_type=F32)
        m_old = m_ref[...]
        m_new = jnp.maximum(m_old, jnp.max(s, axis=-1, keepdims=True))
        p = jnp.exp2(s - m_new).astype(BF16)
        acc_ref[...] = jnp.exp2(m_old - m_new) * acc_ref[...] + jnp.dot(p, v, preferred_element_type=F32)
        m_ref[...] = m_new
        return carry

    lax.fori_loop(0, nk, step, 0)
    acc = acc_ref[...]
    o_ref[0] = (acc / acc[:, SUM_LANE:SUM_LANE + 1]).astype(o_ref.dtype)


def _attention(q, k, v, tq, tk):
    bsz, lp, wide = q.shape
    return pl.pallas_call(
        functools.partial(_attn_kernel, tk),
        grid=(bsz, N_HEADS, lp // tq),
        in_specs=[
            pl.BlockSpec((1, tq, HEAD_SLOT), lambda b, h, i: (b, i, h)),
            pl.BlockSpec((1, lp, HEAD_SLOT), lambda b, h, i: (b, 0, h)),
            pl.BlockSpec((1, lp, HEAD_SLOT), lambda b, h, i: (b, 0, h)),
        ],
        out_specs=pl.BlockSpec((1, tq, HEAD_SLOT), lambda b, h, i: (b, i, h)),
        out_shape=jax.ShapeDtypeStruct((bsz, lp, wide), BF16),
        scratch_shapes=[pltpu.VMEM((tq, 1), F32), pltpu.VMEM((tq, HEAD_SLOT), F32)],
        compiler_params=_params(("parallel", "parallel", "arbitrary")),
        name="attention",
    )(q, k, v)


def _slot_cols(w, per_head, pieces):
    k = w.shape[0]
    wh = w.reshape(k, N_HEADS, per_head)
    out = jnp.zeros((k, N_HEADS, HEAD_SLOT), w.dtype)
    for dst, src, width, sign in pieces:
        out = out.at[:, :, dst:dst + width].set(sign * wh[:, :, src:src + width])
    return out.reshape(k, N_HEADS * HEAD_SLOT)


def _mla_weights(wq_b, wkv_a, wkv_b, wo):
    half = QK_ROPE // 2
    qk = QK_NOPE + QK_ROPE
    q_scale = (qk ** -0.5) * math.log2(math.e)
    wq1 = _slot_cols(wq_b, qk, [(0, 0, QK_NOPE, 1.0), (QK_NOPE, QK_NOPE, QK_ROPE, 1.0)]) * q_scale
    wq2 = _slot_cols(wq_b, qk, [(QK_NOPE, QK_NOPE + half, half, -1.0),
                                (QK_NOPE + half, QK_NOPE, half, 1.0)]) * q_scale
    wc = wkv_a[:, :KV_RANK]
    pe = wkv_a[:, KV_RANK:]
    zeros = lambda n: jnp.zeros((D_MODEL, n), wkv_a.dtype)
    tail = HEAD_SLOT - QK_NOPE - QK_ROPE
    wk1 = jnp.concatenate([zeros(QK_NOPE), pe, zeros(tail)], axis=1)
    wk2 = jnp.concatenate([zeros(QK_NOPE), -pe[:, half:], pe[:, :half], zeros(tail)], axis=1)
    wkk = _slot_cols(wkv_b, QK_NOPE + V_HEAD, [(0, 0, QK_NOPE, 1.0)])
    wkv = _slot_cols(wkv_b, QK_NOPE + V_HEAD, [(0, QK_NOPE, V_HEAD, 1.0)])
    wo_slot = jnp.zeros((N_HEADS, HEAD_SLOT, D_MODEL), wo.dtype)
    wo_slot = wo_slot.at[:, :V_HEAD, :].set(wo.reshape(N_HEADS, V_HEAD, D_MODEL))
    wo_slot = wo_slot.reshape(N_HEADS * HEAD_SLOT, D_MODEL)
    cast = lambda a: a.astype(BF16)
    return tuple(map(cast, (wq1, wq2, wc, wk1, wk2, wkk, wkv, wo_slot)))


def _rope_slot_tables(lp):
    pos = jnp.arange(lp, dtype=F32)
    inv = 1.0 / (ROPE_THETA ** (jnp.arange(0, QK_ROPE, 2, dtype=F32) / QK_ROPE))
    ang = pos[:, None] * inv[None, :]
    cos, sin = jnp.cos(ang), jnp.sin(ang)
    tail = HEAD_SLOT - QK_NOPE - QK_ROPE
    cos_t = jnp.concatenate([jnp.ones((lp, QK_NOPE), F32), cos, cos, jnp.zeros((lp, tail), F32)], axis=1)
    sin_t = jnp.concatenate([jnp.zeros((lp, QK_NOPE), F32), sin, sin, jnp.zeros((lp, tail), F32)], axis=1)
    return cos_t, sin_t


def _filter_features(l_true, lp):
    t = jnp.arange(lp, dtype=F32) / max(l_true - 1, 1)
    bands = (FILTER_EMB - 1) // 2
    freqs = jnp.linspace(1e-4, bands - 1, bands, dtype=F32)
    w = 2.0 * math.pi * jnp.arange(lp, dtype=F32) / l_true
    ang = w[:, None] * freqs[None, :]
    z = jnp.concatenate([t[:, None], jnp.cos(ang), -jnp.sin(ang)], axis=-1)
    return jnp.pad(z, ((0, 0), (0, FILTER_EMB_PAD - FILTER_EMB)))


def _divisor_tile(total, target, multiple):
    best = None
    for d in range(multiple, min(total, target) + 1, multiple):
        if total % d == 0:
            best = d
    assert best is not None, (total, target, multiple)
    return best


SINGLE_KV_TILE_MAX = 2304
MXU_COLS = 256


def _tiling(bsz, l_true):
    if _round_up(l_true, LANES) <= SINGLE_KV_TILE_MAX:
        lp = _round_up(l_true, LANES)
        tk = lp
    else:
        lp = _round_up(l_true, MXU_COLS)
        tk = _divisor_tile(lp, 1280, MXU_COLS)
    return dict(
        lp=lp, tk=tk,
        tl=_divisor_tile(lp, 640, HALO),
        tq=_divisor_tile(lp, 1280, 2 * SUBLANES),
        tm=_divisor_tile(bsz * lp, 1088, 2 * SUBLANES),
        tmlp=_divisor_tile(bsz * lp, 1280, 2 * SUBLANES),
    )


def _trunk(x, p):
    bsz, seq, _ = x.shape
    l_true = seq + N_META
    cfg = _tiling(bsz, l_true)
    lp = cfg["lp"]
    rows = bsz * lp

    meta = jnp.broadcast_to(p["meta_tokens"][None], (bsz, N_META, D_MODEL))
    h = jnp.concatenate([meta, x, jnp.zeros((bsz, lp - l_true, D_MODEL), F32)], axis=1)
    h = h.reshape(rows, D_MODEL)

    proj = _mm_bias(h, p["ev_w_in"], p["ev_b_in"], cfg["tm"]).reshape(bsz, lp, D_IN_EVEN)
    x0, z, y_b = _conv_mix(proj, l_true, cfg["tl"], p["ev_short_w"], p["ev_short_b"],
                           p["cf_dw_w"], p["cf_dw_b"], p["cf_ln_g"], p["cf_ln_b"])

    kz = lp // FFT_N2
    kzp = _round_up(kz, SUBLANES)
    n1 = _round_up(-(-(2 * l_true - 1) // FFT_N2), SUBLANES)
    fa, ga, mmat, mmat_t = _dft_tables(n1, kz, kzp)
    cols = FFT_N2 * D_HYENA
    tn = 2048

    zf = _filter_features(l_true, lp)
    taps, sumsq = _hyena_filter(zf, l_true, cfg["tl"], p["hy_w1"], p["hy_b1"], p["hy_freq1"], p["hy_w2"],
                                p["hy_b2"], p["hy_freq2"], p["hy_w3"], p["hy_decay"])
    rs = lax.rsqrt(sumsq)
    kf = _dft_b_filter(mmat, _dft_a(fa, taps.reshape(2, kz, cols), tn), rs)

    za = _dft_a(fa, z.reshape(bsz, kz, cols), tn)
    wb = _dft_b(mmat, mmat_t, za, kf)
    d_tiled = jnp.tile(p["hy_skip_d"], (1, FFT_N2))
    y_a = _dft_c(ga, wb, x0.reshape(bsz, kz, cols), z.reshape(bsz, kz, cols), d_tiled, tn)
    y_a = y_a.reshape(rows, D_HYENA)

    h = _mm_res_ln([y_a, y_b.reshape(rows, D_CONF)], [p["ev_w_out_a"], p["ev_w_out_b"]], p["ev_b_out"],
                   h, p["ln1_g"][0], p["ln1_b"][0], cfg["tm"], "out_proj_even")
    h = _mlp(h, p["mlp_w1"][0], p["mlp_w2"][0], p["ln2_g"][0], p["ln2_b"][0], cfg["tmlp"], 1024)

    cos_t, sin_t = _rope_slot_tables(lp)
    q, k, v = _mla_proj(h.reshape(bsz, lp, D_MODEL), l_true, cfg["tl"], cos_t, sin_t,
                        p["mla_wq_a"], p["mla_q_norm"], p["wq1"], p["wq2"], p["wc"], p["mla_kv_norm"],
                        p["wk1"], p["wk2"], p["wkk"], p["wkv"])
    o = _attention(q, k, v, cfg["tq"], cfg["tk"]).reshape(rows, N_HEADS * HEAD_SLOT)
    h = _mm_res_ln([o], [p["wo_slot"]], jnp.zeros((1, D_MODEL), F32), h, p["ln1_g"][1], p["ln1_b"][1],
                   cfg["tm"], "out_proj_odd")
    h = _mlp(h, p["mlp_w1"][1], p["mlp_w2"][1], p["ln2_g"][1], p["ln2_b"][1], cfg["tmlp"], 1024)

    return h.reshape(bsz, lp, D_MODEL)[:, N_META:l_true]


def kernel(x_prompt, x_sample, meta_tokens, ev_w_in, ev_b_in, ev_short_w, ev_short_b, hy_w1, hy_b1, hy_freq1,
           hy_w2, hy_b2, hy_freq2, hy_w3, hy_decay, hy_skip_d, cf_dw_w, cf_dw_b, cf_ln_g, cf_ln_b, ev_w_out,
           ev_b_out, mla_wq_a, mla_q_norm, mla_wq_b, mla_wkv_a, mla_kv_norm, mla_wkv_b, mla_wo, ln1_g, ln1_b,
           mlp_w1, mlp_w2, ln2_g, ln2_b):
    row = lambda a: a.reshape(1, -1)
    wq1, wq2, wc, wk1, wk2, wkk, wkv, wo_slot = _mla_weights(mla_wq_b[0], mla_wkv_a[0], mla_wkv_b[0], mla_wo[0])
    p = dict(
        meta_tokens=meta_tokens,
        ev_w_in=ev_w_in[0].astype(BF16), ev_b_in=row(ev_b_in[0]),
        ev_short_w=ev_short_w[0], ev_short_b=row(ev_short_b[0]),
        hy_w1=jnp.pad(hy_w1[0], ((0, 0), (0, FILTER_EMB_PAD - FILTER_EMB), (0, 0))),
        hy_b1=hy_b1[0][:, None, :], hy_freq1=hy_freq1[0][:, None, :],
        hy_w2=hy_w2[0], hy_b2=hy_b2[0][:, None, :], hy_freq2=hy_freq2[0][:, None, :],
        hy_w3=hy_w3[0], hy_decay=hy_decay[0][:, None, :], hy_skip_d=row(hy_skip_d[0]),
        cf_dw_w=cf_dw_w[0], cf_dw_b=row(cf_dw_b[0]), cf_ln_g=row(cf_ln_g[0]), cf_ln_b=row(cf_ln_b[0]),
        ev_w_out_a=ev_w_out[0, :D_HYENA].astype(BF16), ev_w_out_b=ev_w_out[0, D_HYENA:].astype(BF16),
        ev_b_out=row(ev_b_out[0]),
        mla_wq_a=mla_wq_a[0].astype(BF16), mla_q_norm=row(mla_q_norm[0]), mla_kv_norm=row(mla_kv_norm[0]),
        wq1=wq1, wq2=wq2, wc=wc, wk1=wk1, wk2=wk2, wkk=wkk, wkv=wkv, wo_slot=wo_slot,
        ln1_g=ln1_g[:, None, :], ln1_b=ln1_b[:, None, :], ln2_g=ln2_g[:, None, :], ln2_b=ln2_b[:, None, :],
        mlp_w1=mlp_w1.astype(BF16), mlp_w2=mlp_w2.astype(BF16),
    )
    return (_trunk(x_prompt, p), _trunk(x_sample, p))
```

```python
import functools
import math

import jax
import jax.numpy as jnp
import numpy as np
from jax import lax
from jax.experimental import pallas as pl
from jax.experimental.pallas import tpu as pltpu

F32 = jnp.float32
BF16 = jnp.bfloat16

D_MODEL = 1024
N_META = 16
D_HYENA = 512
D_CONF = 512
D_IN_EVEN = 3 * D_HYENA + 2 * D_CONF
CONF_K = 31
FILTER_EMB = 33
FILTER_EMB_PAD = 40
FILTER_HIDDEN = 64
N_HEADS = 16
QK_NOPE = 64
QK_ROPE = 32
V_HEAD = 64
Q_RANK = 384
KV_RANK = 256
ROPE_THETA = 10000.0
D_FF = 4096
DEPTH = 2
DN_ALPHA = (2 * DEPTH) ** 0.25
LN_EPS = 1e-5
RMS_EPS = 1e-6

LANES = 128
SUBLANES = 8
HEAD_SLOT = LANES
HALO = 16
VMEM_LIMIT = 56 * 1024 * 1024

PAD_LANE = QK_NOPE + QK_ROPE
SUM_LANE = V_HEAD
PAD_SCORE = -1e30
FFT_N2 = 128

HIGHEST = lax.Precision.HIGHEST


def _params(sem, vmem=VMEM_LIMIT):
    return pltpu.CompilerParams(dimension_semantics=sem, vmem_limit_bytes=vmem)


def _round_up(x, m):
    return -(-x // m) * m


def _layer_norm(x, g, b):
    mu = jnp.mean(x, axis=-1, keepdims=True)
    xc = x - mu
    var = jnp.mean(xc * xc, axis=-1, keepdims=True)
    return xc * lax.rsqrt(var + LN_EPS) * g + b


def _mm_bias_kernel(x_ref, w_ref, b_ref, o_ref):
    acc = jnp.dot(x_ref[...].astype(BF16), w_ref[...], preferred_element_type=F32)
    o_ref[...] = (acc + b_ref[...]).astype(o_ref.dtype)


def _mm_bias(x, w, b, tm, out_dtype=F32):
    rows, k = x.shape
    n = w.shape[1]
    return pl.pallas_call(
        _mm_bias_kernel,
        grid=(pl.cdiv(rows, tm),),
        in_specs=[
            pl.BlockSpec((tm, k), lambda i: (i, 0)),
            pl.BlockSpec((k, n), lambda i: (0, 0)),
            pl.BlockSpec((1, n), lambda i: (0, 0)),
        ],
        out_specs=pl.BlockSpec((tm, n), lambda i: (i, 0)),
        out_shape=jax.ShapeDtypeStruct((rows, n), out_dtype),
        compiler_params=_params(("parallel",)),
        name="in_proj",
    )(x, w, b)


def _mm_res_ln_kernel(n_in, *refs):
    a_refs = refs[:n_in]
    w_refs = refs[n_in:2 * n_in]
    bias_ref, res_ref, g_ref, b_ref, o_ref = refs[2 * n_in:]
    acc = bias_ref[...] + DN_ALPHA * res_ref[...]
    for a_ref, w_ref in zip(a_refs, w_refs):
        acc = acc + jnp.dot(a_ref[...].astype(BF16), w_ref[...], preferred_element_type=F32)
    o_ref[...] = _layer_norm(acc, g_ref[...], b_ref[...])


def _mm_res_ln(a_list, w_list, bias, res, g, b, tm, name):
    rows = res.shape[0]
    n_in = len(a_list)
    in_specs = [pl.BlockSpec((tm, a.shape[1]), lambda i: (i, 0)) for a in a_list]
    in_specs += [pl.BlockSpec(w.shape, lambda i: (0, 0)) for w in w_list]
    in_specs += [
        pl.BlockSpec((1, D_MODEL), lambda i: (0, 0)),
        pl.BlockSpec((tm, D_MODEL), lambda i: (i, 0)),
        pl.BlockSpec((1, D_MODEL), lambda i: (0, 0)),
        pl.BlockSpec((1, D_MODEL), lambda i: (0, 0)),
    ]
    return pl.pallas_call(
        functools.partial(_mm_res_ln_kernel, n_in),
        grid=(pl.cdiv(rows, tm),),
        in_specs=in_specs,
        out_specs=pl.BlockSpec((tm, D_MODEL), lambda i: (i, 0)),
        out_shape=jax.ShapeDtypeStruct((rows, D_MODEL), F32),
        compiler_params=_params(("parallel",)),
        name=name,
    )(*a_list, *w_list, bias, res, g, b)


def _mlp_kernel(x_ref, w1_ref, w2_ref, g_ref, b_ref, o_ref, acc_ref):
    k = pl.program_id(1)

    @pl.when(k == 0)
    def _():
        acc_ref[...] = jnp.zeros_like(acc_ref)

    hid = jnp.dot(x_ref[...].astype(BF16), w1_ref[...], preferred_element_type=F32)
    hid = jnp.maximum(hid, 0.0)
    hid = (hid * hid).astype(BF16)
    acc_ref[...] += jnp.dot(hid, w2_ref[...], preferred_element_type=F32)

    @pl.when(k == pl.num_programs(1) - 1)
    def _():
        o_ref[...] = _layer_norm(DN_ALPHA * x_ref[...] + acc_ref[...], g_ref[...], b_ref[...])


def _mlp(x, w1, w2, g, b, tm, tk):
    rows = x.shape[0]
    return pl.pallas_call(
        _mlp_kernel,
        grid=(pl.cdiv(rows, tm), D_FF // tk),
        in_specs=[
            pl.BlockSpec((tm, D_MODEL), lambda i, k: (i, 0)),
            pl.BlockSpec((D_MODEL, tk), lambda i, k: (0, k)),
            pl.BlockSpec((tk, D_MODEL), lambda i, k: (k, 0)),
            pl.BlockSpec((1, D_MODEL), lambda i, k: (0, 0)),
            pl.BlockSpec((1, D_MODEL), lambda i, k: (0, 0)),
        ],
        out_specs=pl.BlockSpec((tm, D_MODEL), lambda i, k: (i, 0)),
        out_shape=jax.ShapeDtypeStruct((rows, D_MODEL), F32),
        scratch_shapes=[pltpu.VMEM((tm, D_MODEL), F32)],
        compiler_params=_params(("parallel", "arbitrary")),
        name="mlp",
    )(x, w1, w2, g, b)


def _conv_mix_kernel(l_true, tl, rc,
                     main_ref, prev_ref, next_ref, sw_ref, sb_ref, dw_ref, db_ref, g_ref, b_ref,
                     x0_ref, z_ref, yb_ref, ext_ref, glu_ref):
    i = pl.program_id(1)
    row0 = i * tl - HALO

    def valid_rows(start, n):
        g = start + lax.broadcasted_iota(jnp.int32, (n, 1), 0)
        return (g >= 0) & (g < l_true)

    ext_ref[0:HALO, :] = jnp.where(valid_rows(row0, HALO), prev_ref[0], 0.0)
    ext_ref[HALO:HALO + tl, :] = jnp.where(valid_rows(row0 + HALO, tl), main_ref[0], 0.0)
    ext_ref[HALO + tl:, :] = jnp.where(valid_rows(row0 + HALO + tl, HALO), next_ref[0], 0.0)

    a = ext_ref[:, 3 * D_HYENA:3 * D_HYENA + D_CONF]
    gate = ext_ref[:, 3 * D_HYENA + D_CONF:]
    glu_ref[...] = a * jax.nn.sigmoid(gate)

    sw = sw_ref[...]
    sb = sb_ref[...]
    dw = dw_ref[...]
    half = (CONF_K - 1) // 2

    def chunk(c, carry):
        r0 = pl.multiple_of(c * rc, SUBLANES)
        rows_ok = valid_rows(i * tl + r0, rc)

        win = ext_ref[pl.ds(r0 + HALO - SUBLANES, rc + 2 * SUBLANES), 0:3 * D_HYENA]
        hy = sb
        for k in range(3):
            off = SUBLANES + k - 1
            hy = hy + sw[k:k + 1, :] * win[off:off + rc, :]
        x0_ref[0, pl.ds(r0, rc), :] = hy[:, 0:D_HYENA]
        zz = hy[:, D_HYENA:2 * D_HYENA] * hy[:, 2 * D_HYENA:]
        z_ref[0, pl.ds(r0, rc), :] = jnp.where(rows_ok, zz, 0.0)

        gwin = glu_ref[pl.ds(r0, rc + 2 * HALO), :]
        acc = db_ref[...]
        for k in range(CONF_K):
            off = HALO + k - half
            acc = acc + dw[k:k + 1, :] * gwin[off:off + rc, :]
        y = _layer_norm(acc, g_ref[...], b_ref[...])
        yb_ref[0, pl.ds(r0, rc), :] = y * jax.nn.sigmoid(y)
        return carry

    lax.fori_loop(0, tl // rc, chunk, 0)


def _conv_mix(proj, l_true, tl, short_w, short_b, dw_w, dw_b, ln_g, ln_b):
    bsz, lp, _ = proj.shape
    nt = lp // tl
    hb = tl // HALO
    rc = _divisor_tile(tl, 64, SUBLANES)
    out_sds = jax.ShapeDtypeStruct((bsz, lp, D_HYENA), F32)
    row_spec = lambda shape: pl.BlockSpec(shape, lambda b, i: (0, 0))
    return pl.pallas_call(
        functools.partial(_conv_mix_kernel, l_true, tl, rc),
        grid=(bsz, nt),
        in_specs=[
            pl.BlockSpec((1, tl, D_IN_EVEN), lambda b, i: (b, i, 0)),
            pl.BlockSpec((1, HALO, D_IN_EVEN), lambda b, i: (b, jnp.maximum(i * hb - 1, 0), 0)),
            pl.BlockSpec((1, HALO, D_IN_EVEN), lambda b, i: (b, jnp.minimum((i + 1) * hb, nt * hb - 1), 0)),
            row_spec((3, 3 * D_HYENA)),
            row_spec((1, 3 * D_HYENA)),
            row_spec((CONF_K, D_CONF)),
            row_spec((1, D_CONF)),
            row_spec((1, D_CONF)),
            row_spec((1, D_CONF)),
        ],
        out_specs=[pl.BlockSpec((1, tl, D_HYENA), lambda b, i: (b, i, 0))] * 3,
        out_shape=[out_sds, out_sds, out_sds],
        scratch_shapes=[
            pltpu.VMEM((tl + 2 * HALO, D_IN_EVEN), F32),
            pltpu.VMEM((tl + 2 * HALO, D_CONF), F32),
        ],
        compiler_params=_params(("parallel", "parallel")),
        name="conv_mix",
    )(proj, proj, proj, short_w, short_b, dw_w, dw_b, ln_g, ln_b)


def _filter_kernel(l_true, tl, zf_ref, w1_ref, b1_ref, f1_ref, w2_ref, b2_ref, f2_ref, w3_ref, dec_ref,
                   h_ref, ss_ref):
    i = pl.program_id(0)

    @pl.when(i == 0)
    def _():
        ss_ref[...] = jnp.zeros_like(ss_ref)

    zf = zf_ref[...]
    t = zf[:, 0:1]
    g = i * tl + lax.broadcasted_iota(jnp.int32, (tl, 1), 0)
    total = jnp.zeros((1, D_HYENA), F32)
    for d in range(2):
        a = jnp.dot(zf, w1_ref[d], precision=HIGHEST, preferred_element_type=F32) + b1_ref[d]
        a = jnp.sin(f1_ref[d] * a)
        a = jnp.dot(a, w2_ref[d], precision=HIGHEST, preferred_element_type=F32) + b2_ref[d]
        a = jnp.sin(f2_ref[d] * a)
        hh = jnp.dot(a, w3_ref[d], precision=HIGHEST, preferred_element_type=F32)
        hh = hh * jnp.exp(-t * dec_ref[d])
        keep = (g < l_true) & (g >= d)
        hh = jnp.where(keep, hh, 0.0)
        h_ref[d] = hh
        total = total + jnp.sum(hh * hh, axis=0, keepdims=True)
    ss_ref[...] += total


def _hyena_filter(zf, l_true, tl, w1, b1, f1, w2, b2, f2, w3, decay):
    lp = zf.shape[0]
    full = lambda shape: pl.BlockSpec(shape, lambda i: (0,) * len(shape))
    return pl.pallas_call(
        functools.partial(_filter_kernel, l_true, tl),
        grid=(lp // tl,),
        in_specs=[
            pl.BlockSpec((tl, FILTER_EMB_PAD), lambda i: (i, 0)),
            full((2, FILTER_EMB_PAD, FILTER_HIDDEN)),
            full((2, 1, FILTER_HIDDEN)),
            full((2, 1, FILTER_HIDDEN)),
            full((2, FILTER_HIDDEN, FILTER_HIDDEN)),
            full((2, 1, FILTER_HIDDEN)),
            full((2, 1, FILTER_HIDDEN)),
            full((2, FILTER_HIDDEN, D_HYENA)),
            full((2, 1, D_HYENA)),
        ],
        out_specs=[
            pl.BlockSpec((2, tl, D_HYENA), lambda i: (0, i, 0)),
            pl.BlockSpec((1, D_HYENA), lambda i: (0, 0)),
        ],
        out_shape=[
            jax.ShapeDtypeStruct((2, lp, D_HYENA), F32),
            jax.ShapeDtypeStruct((1, D_HYENA), F32),
        ],
        compiler_params=_params(("arbitrary",)),
        name="hyena_filter",
    )(zf, w1, b1, f1, w2, b2, f2, w3, decay)


S2_BLOCK = SUBLANES


def _dft_a_kernel(kz, kzp, f_ref, x_ref, o_ref):
    for j in range(S2_BLOCK):
        x = x_ref[0, :, j, :]
        if kzp != kz:
            x = jnp.concatenate([x, jnp.zeros((kzp - kz, x.shape[1]), F32)], axis=0)
        o_ref[0, :, j, :] = jnp.dot(f_ref[...], x, precision=HIGHEST, preferred_element_type=F32)


def _dft_a(fmat, x):
    bsz, kz, n2, ch = x.shape
    m, kzp = fmat.shape
    return pl.pallas_call(
        functools.partial(_dft_a_kernel, kz, kzp),
        grid=(bsz, n2 // S2_BLOCK),
        in_specs=[
            pl.BlockSpec((m, kzp), lambda b, j: (0, 0)),
            pl.BlockSpec((1, kz, S2_BLOCK, ch), lambda b, j: (b, 0, j, 0)),
        ],
        out_specs=pl.BlockSpec((1, m, S2_BLOCK, ch), lambda b, j: (b, 0, j, 0)),
        out_shape=jax.ShapeDtypeStruct((bsz, m, n2, ch), F32),
        compiler_params=_params(("parallel", "parallel")),
        name="dft_stage_a",
    )(fmat, x)


def _twiddle(ar, ai, twr, twi):
    return ar * twr - ai * twi, ar * twi + ai * twr


def _dft_b_filter_kernel(m_ref, twr_ref, twi_ref, a_ref, rs_ref, kf_ref):
    twr, twi = twr_ref[0], twi_ref[0]
    fwd = jnp.concatenate(_twiddle(a_ref[0, 0, 0], a_ref[0, 1, 0], twr, twi), axis=0)
    bwd = jnp.concatenate(_twiddle(a_ref[1, 0, 0], a_ref[1, 1, 0], twr, twi), axis=0)
    zf = jnp.dot(m_ref[...], fwd, precision=HIGHEST, preferred_element_type=F32)
    zb = jnp.dot(m_ref[...], bwd, precision=HIGHEST, preferred_element_type=F32)
    rs = rs_ref[...]
    kf_ref[0, 0] = (zf[:FFT_N2] + zb[:FFT_N2]) * rs
    kf_ref[1, 0] = (zf[FFT_N2:] - zb[FFT_N2:]) * rs


def _dft_b_filter(mc, twr, twi, a, rs):
    n1 = twr.shape[0]
    a5 = a.reshape(2, 2, n1, FFT_N2, D_HYENA)
    tw_spec = pl.BlockSpec((1, FFT_N2, 1), lambda f: (f, 0, 0))
    return pl.pallas_call(
        _dft_b_filter_kernel,
        grid=(n1,),
        in_specs=[
            pl.BlockSpec((2 * FFT_N2, 2 * FFT_N2), lambda f: (0, 0)),
            tw_spec, tw_spec,
            pl.BlockSpec((2, 2, 1, FFT_N2, D_HYENA), lambda f: (0, 0, f, 0, 0)),
            pl.BlockSpec((1, D_HYENA), lambda f: (0, 0)),
        ],
        out_specs=pl.BlockSpec((2, 1, FFT_N2, D_HYENA), lambda f: (0, f, 0, 0)),
        out_shape=jax.ShapeDtypeStruct((2, n1, FFT_N2, D_HYENA), F32),
        compiler_params=_params(("parallel",)),
        name="dft_filter_spectrum",
    )(mc, twr, twi, a5, rs)


def _dft_b_kernel(m_ref, mt_ref, twr_ref, twi_ref, a_ref, kf_ref, o_ref):
    twr, twi = twr_ref[0], twi_ref[0]
    x = jnp.concatenate(_twiddle(a_ref[0, 0, 0], a_ref[0, 1, 0], twr, twi), axis=0)
    z = jnp.dot(m_ref[...], x, precision=HIGHEST, preferred_element_type=F32)
    y = jnp.concatenate(_twiddle(z[:FFT_N2], z[FFT_N2:], kf_ref[0, 0], kf_ref[1, 0]), axis=0)
    w = jnp.dot(mt_ref[...], y, precision=HIGHEST, preferred_element_type=F32)
    o_ref[0, 0, 0], o_ref[0, 1, 0] = _twiddle(w[:FFT_N2], w[FFT_N2:], twr, -twi)


def _dft_b(mc, mct, twr, twi, a, kf):
    bsz = a.shape[0]
    n1 = twr.shape[0]
    a5 = a.reshape(bsz, 2, n1, FFT_N2, D_HYENA)
    mat_spec = pl.BlockSpec((2 * FFT_N2, 2 * FFT_N2), lambda b, f: (0, 0))
    tw_spec = pl.BlockSpec((1, FFT_N2, 1), lambda b, f: (f, 0, 0))
    out = pl.pallas_call(
        _dft_b_kernel,
        grid=(bsz, n1),
        in_specs=[
            mat_spec, mat_spec, tw_spec, tw_spec,
            pl.BlockSpec((1, 2, 1, FFT_N2, D_HYENA), lambda b, f: (b, 0, f, 0, 0)),
            pl.BlockSpec((2, 1, FFT_N2, D_HYENA), lambda b, f: (0, f, 0, 0)),
        ],
        out_specs=pl.BlockSpec((1, 2, 1, FFT_N2, D_HYENA), lambda b, f: (b, 0, f, 0, 0)),
        out_shape=jax.ShapeDtypeStruct((bsz, 2, n1, FFT_N2, D_HYENA), F32),
        compiler_params=_params(("parallel", "parallel")),
        name="dft_stage_b",
    )(mc, mct, twr, twi, a5, kf)
    return out.reshape(bsz, 2 * n1, FFT_N2, D_HYENA)


def _dft_c_kernel(kz, g_ref, w_ref, x0_ref, z_ref, d_ref, o_ref):
    for j in range(S2_BLOCK):
        y = jnp.dot(g_ref[...], w_ref[0, :, j, :], precision=HIGHEST, preferred_element_type=F32)
        o_ref[0, :, j, :] = x0_ref[0, :, j, :] * (y[:kz] + z_ref[0, :, j, :] * d_ref[...])


def _dft_c(gmat, w, x0, z, skip_d):
    bsz, kz, n2, ch = x0.shape
    kzp, m = gmat.shape
    seq_spec = pl.BlockSpec((1, kz, S2_BLOCK, ch), lambda b, j: (b, 0, j, 0))
    return pl.pallas_call(
        functools.partial(_dft_c_kernel, kz),
        grid=(bsz, n2 // S2_BLOCK),
        in_specs=[
            pl.BlockSpec((kzp, m), lambda b, j: (0, 0)),
            pl.BlockSpec((1, m, S2_BLOCK, ch), lambda b, j: (b, 0, j, 0)),
            seq_spec, seq_spec,
            pl.BlockSpec((1, ch), lambda b, j: (0, 0)),
        ],
        out_specs=seq_spec,
        out_shape=jax.ShapeDtypeStruct((bsz, kz, n2, ch), F32),
        compiler_params=_params(("parallel", "parallel")),
        name="dft_stage_c",
    )(gmat, w, x0, z, skip_d)


def _dft_tables(n1, kz, kzp):
    n2 = FFT_N2
    n = n1 * n2
    f1 = np.arange(n1)
    s1 = np.arange(kzp)
    ang = (2.0 * np.pi / n1) * ((f1[:, None] * s1[None, :]) % n1)
    live = (s1 < kz)[None, :]
    fa = np.concatenate([np.where(live, np.cos(ang), 0.0), np.where(live, -np.sin(ang), 0.0)], axis=0)
    ga = np.concatenate([np.cos(ang).T, -np.sin(ang).T], axis=1) * np.where(s1 < kz, 1.0 / n, 0.0)[:, None]
    idx = np.arange(n2)
    ang2 = (2.0 * np.pi / n2) * ((idx[:, None] * idx[None, :]) % n2)
    cr, ci = np.cos(ang2), -np.sin(ang2)
    mc = np.block([[cr, -ci], [ci, cr]])
    mct = np.block([[cr, ci], [-ci, cr]])
    ang3 = (2.0 * np.pi / n) * ((f1[:, None] * idx[None, :]) % n)
    twr, twi = np.cos(ang3)[:, :, None], -np.sin(ang3)[:, :, None]
    return tuple(jnp.asarray(t, F32) for t in (fa, ga, mc, mct, twr, twi))


def _rms(x, g):
    return x * lax.rsqrt(jnp.mean(x * x, axis=-1, keepdims=True) + RMS_EPS) * g


def _mla_proj_kernel(l_true, tm,
                     x_ref, cos_ref, sin_ref, wqa_ref, qn_ref, wq1_ref, wq2_ref, wc_ref, kn_ref,
                     wk1_ref, wk2_ref, wkk_ref, wkv_ref, q_ref, k_ref, v_ref):
    i = pl.program_id(1)
    x = x_ref[0].astype(BF16)
    cos = cos_ref[...]
    sin = sin_ref[...]
    lane = lax.broadcasted_iota(jnp.int32, (1, HEAD_SLOT), 1)

    cq = _rms(jnp.dot(x, wqa_ref[...], preferred_element_type=F32), qn_ref[...]).astype(BF16)
    q1 = jnp.dot(cq, wq1_ref[...], preferred_element_type=F32)
    q2 = jnp.dot(cq, wq2_ref[...], preferred_element_type=F32)
    q_one = jnp.where(lane == PAD_LANE, 1.0, 0.0)

    ckv = _rms(jnp.dot(x, wc_ref[...], preferred_element_type=F32), kn_ref[...]).astype(BF16)
    kn = jnp.dot(ckv, wkk_ref[...], preferred_element_type=F32)
    vv = jnp.dot(ckv, wkv_ref[...], preferred_element_type=F32)
    k1 = jnp.dot(x, wk1_ref[...], preferred_element_type=F32)
    k2 = jnp.dot(x, wk2_ref[...], preferred_element_type=F32)
    rows = i * tm + lax.broadcasted_iota(jnp.int32, (tm, 1), 0)
    k_pe = k1 * cos + k2 * sin + jnp.where((rows >= l_true) & (lane == PAD_LANE), PAD_SCORE, 0.0)
    v_one = jnp.where(lane == SUM_LANE, 1.0, 0.0)

    for h in range(N_HEADS):
        sl = slice(h * HEAD_SLOT, (h + 1) * HEAD_SLOT)
        q_ref[0, :, sl] = (q1[:, sl] * cos + q2[:, sl] * sin + q_one).astype(BF16)
        k_ref[0, :, sl] = (kn[:, sl] + k_pe).astype(BF16)
        v_ref[0, :, sl] = (vv[:, sl] + v_one).astype(BF16)


def _mla_proj(h, l_true, tm, cos_t, sin_t, wqa, qn, wq1, wq2, wc, kn, wk1, wk2, wkk, wkv):
    bsz, lp, _ = h.shape
    wide = N_HEADS * HEAD_SLOT
    full = lambda a: pl.BlockSpec(a.shape, lambda b, i: (0,) * a.ndim)
    out_sds = jax.ShapeDtypeStruct((bsz, lp, wide), BF16)
    weights = (wqa, qn, wq1, wq2, wc, kn, wk1, wk2, wkk, wkv)
    return pl.pallas_call(
        functools.partial(_mla_proj_kernel, l_true, tm),
        grid=(bsz, lp // tm),
        in_specs=[
            pl.BlockSpec((1, tm, D_MODEL), lambda b, i: (b, i, 0)),
            pl.BlockSpec((tm, HEAD_SLOT), lambda b, i: (i, 0)),
            pl.BlockSpec((tm, HEAD_SLOT), lambda b, i: (i, 0)),
        ] + [full(w) for w in weights],
        out_specs=[pl.BlockSpec((1, tm, wide), lambda b, i: (b, i, 0))] * 3,
        out_shape=[out_sds, out_sds, out_sds],
        compiler_params=_params(("parallel", "parallel")),
        name="mla_proj",
    )(h, cos_t, sin_t, *weights)


def _attn_kernel(tk, tr, q_ref, k_ref, v_ref, o_ref, s0_ref, s1_ref, m_ref, acc_ref):
    tq = q_ref.shape[1]
    nk = k_ref.shape[1] // tk
    s_refs = (s0_ref, s1_ref)
    row_blocks = [slice(r * tr, (r + 1) * tr) for r in range(tq // tr)]

    def scores(c, slot):
        k = k_ref[0, pl.ds(pl.multiple_of(c * tk, tk), tk), :]
        for rows in row_blocks:
            s_refs[slot][rows, :] = lax.dot_general(q_ref[0, rows, :], k, (((1,), (1,)), ((), ())),
                                                    preferred_element_type=F32)

    def softmax_pv(c, slot):
        v = v_ref[0, pl.ds(pl.multiple_of(c * tk, tk), tk), :]
        for rows in row_blocks:
            s = s_refs[slot][rows, :]
            m_old = m_ref[rows, :]
            m_new = jnp.maximum(m_old, jnp.max(s, axis=-1, keepdims=True))
            p = jnp.exp2(s - m_new).astype(BF16)
            acc_ref[rows, :] = (jnp.exp2(m_old - m_new) * acc_ref[rows, :]
                                + jnp.dot(p, v, preferred_element_type=F32))
            m_ref[rows, :] = m_new

    def step(c, slot):
        scores(c + 1, 1 - slot)
        softmax_pv(c, slot)

    m_ref[...] = jnp.full_like(m_ref, -jnp.inf)
    acc_ref[...] = jnp.zeros_like(acc_ref)
    scores(0, 0)

    def pair(j, carry):
        step(2 * j, 0)
        step(2 * j + 1, 1)
        return carry

    lax.fori_loop(0, (nk - 1) // 2, pair, 0)
    if (nk - 1) % 2:
        step(nk - 2, (nk - 2) % 2)
    softmax_pv(nk - 1, (nk - 1) % 2)
    acc = acc_ref[...]
    o_ref[0] = (acc / acc[:, SUM_LANE:SUM_LANE + 1]).astype(o_ref.dtype)


def _attention(q, k, v, tq, tk, tr):
    bsz, lp, wide = q.shape
    return pl.pallas_call(
        functools.partial(_attn_kernel, tk, tr),
        grid=(bsz, N_HEADS, lp // tq),
        in_specs=[
            pl.BlockSpec((1, tq, HEAD_SLOT), lambda b, h, i: (b, i, h)),
            pl.BlockSpec((1, lp, HEAD_SLOT), lambda b, h, i: (b, 0, h)),
            pl.BlockSpec((1, lp, HEAD_SLOT), lambda b, h, i: (b, 0, h)),
        ],
        out_specs=pl.BlockSpec((1, tq, HEAD_SLOT), lambda b, h, i: (b, i, h)),
        out_shape=jax.ShapeDtypeStruct((bsz, lp, wide), BF16),
        scratch_shapes=[pltpu.VMEM((tq, tk), F32), pltpu.VMEM((tq, tk), F32),
                        pltpu.VMEM((tq, 1), F32), pltpu.VMEM((tq, HEAD_SLOT), F32)],
        compiler_params=_params(("parallel", "parallel", "arbitrary")),
        name="attention",
    )(q, k, v)


def _slot_cols(w, per_head, pieces):
    k = w.shape[0]
    wh = w.reshape(k, N_HEADS, per_head)
    out = jnp.zeros((k, N_HEADS, HEAD_SLOT), w.dtype)
    for dst, src, width, sign in pieces:
        out = out.at[:, :, dst:dst + width].set(sign * wh[:, :, src:src + width])
    return out.reshape(k, N_HEADS * HEAD_SLOT)


def _mla_weights(wq_b, wkv_a, wkv_b, wo):
    half = QK_ROPE // 2
    qk = QK_NOPE + QK_ROPE
    q_scale = (qk ** -0.5) * math.log2(math.e)
    wq1 = _slot_cols(wq_b, qk, [(0, 0, QK_NOPE, 1.0), (QK_NOPE, QK_NOPE, QK_ROPE, 1.0)]) * q_scale
    wq2 = _slot_cols(wq_b, qk, [(QK_NOPE, QK_NOPE + half, half, -1.0),
                                (QK_NOPE + half, QK_NOPE, half, 1.0)]) * q_scale
    wc = wkv_a[:, :KV_RANK]
    pe = wkv_a[:, KV_RANK:]
    zeros = lambda n: jnp.zeros((D_MODEL, n), wkv_a.dtype)
    tail = HEAD_SLOT - QK_NOPE - QK_ROPE
    wk1 = jnp.concatenate([zeros(QK_NOPE), pe, zeros(tail)], axis=1)
    wk2 = jnp.concatenate([zeros(QK_NOPE), -pe[:, half:], pe[:, :half], zeros(tail)], axis=1)
    wkk = _slot_cols(wkv_b, QK_NOPE + V_HEAD, [(0, 0, QK_NOPE, 1.0)])
    wkv = _slot_cols(wkv_b, QK_NOPE + V_HEAD, [(0, QK_NOPE, V_HEAD, 1.0)])
    wo_slot = jnp.zeros((N_HEADS, HEAD_SLOT, D_MODEL), wo.dtype)
    wo_slot = wo_slot.at[:, :V_HEAD, :].set(wo.reshape(N_HEADS, V_HEAD, D_MODEL))
    wo_slot = wo_slot.reshape(N_HEADS * HEAD_SLOT, D_MODEL)
    cast = lambda a: a.astype(BF16)
    return tuple(map(cast, (wq1, wq2, wc, wk1, wk2, wkk, wkv, wo_slot)))


def _rope_slot_tables(lp):
    pos = jnp.arange(lp, dtype=F32)
    inv = 1.0 / (ROPE_THETA ** (jnp.arange(0, QK_ROPE, 2, dtype=F32) / QK_ROPE))
    ang = pos[:, None] * inv[None, :]
    cos, sin = jnp.cos(ang), jnp.sin(ang)
    tail = HEAD_SLOT - QK_NOPE - QK_ROPE
    cos_t = jnp.concatenate([jnp.ones((lp, QK_NOPE), F32), cos, cos, jnp.zeros((lp, tail), F32)], axis=1)
    sin_t = jnp.concatenate([jnp.zeros((lp, QK_NOPE), F32), sin, sin, jnp.zeros((lp, tail), F32)], axis=1)
    return cos_t, sin_t


def _filter_features(l_true, lp):
    t = jnp.arange(lp, dtype=F32) / max(l_true - 1, 1)
    bands = (FILTER_EMB - 1) // 2
    freqs = jnp.linspace(1e-4, bands - 1, bands, dtype=F32)
    w = 2.0 * math.pi * jnp.arange(lp, dtype=F32) / l_true
    ang = w[:, None] * freqs[None, :]
    z = jnp.concatenate([t[:, None], jnp.cos(ang), -jnp.sin(ang)], axis=-1)
    return jnp.pad(z, ((0, 0), (0, FILTER_EMB_PAD - FILTER_EMB)))


def _divisor_tile(total, target, multiple):
    best = None
    for d in range(multiple, min(total, target) + 1, multiple):
        if total % d == 0:
            best = d
    assert best is not None, (total, target, multiple)
    return best


SHORT_SEQ_MAX = 2304
MXU_COLS = 256


def _tiling(bsz, l_true):
    lp = _round_up(l_true, LANES if l_true <= SHORT_SEQ_MAX else MXU_COLS)
    tk = _divisor_tile(lp, 1280, LANES // 2)
    tq = _divisor_tile(lp, 1280, 2 * SUBLANES)
    return dict(
        lp=lp, tk=tk,
        tl=_divisor_tile(lp, 640, HALO),
        tq=tq, tr=_divisor_tile(tq, 256, 2 * SUBLANES),
        tm=_divisor_tile(bsz * lp, 1088, 2 * SUBLANES),
        tmlp=_divisor_tile(bsz * lp, 1280, 2 * SUBLANES),
    )


def _trunk(x, p):
    bsz, seq, _ = x.shape
    l_true = seq + N_META
    cfg = _tiling(bsz, l_true)
    lp = cfg["lp"]
    rows = bsz * lp

    meta = jnp.broadcast_to(p["meta_tokens"][None], (bsz, N_META, D_MODEL))
    h = jnp.concatenate([meta, x, jnp.zeros((bsz, lp - l_true, D_MODEL), F32)], axis=1)
    h = h.reshape(rows, D_MODEL)

    proj = _mm_bias(h, p["ev_w_in"], p["ev_b_in"], cfg["tm"]).reshape(bsz, lp, D_IN_EVEN)
    x0, z, y_b = _conv_mix(proj, l_true, cfg["tl"], p["ev_short_w"], p["ev_short_b"],
                           p["cf_dw_w"], p["cf_dw_b"], p["cf_ln_g"], p["cf_ln_b"])

    kz = lp // FFT_N2
    kzp = _round_up(kz, SUBLANES)
    n1 = _round_up(-(-(2 * l_true - 1) // FFT_N2), SUBLANES)
    fa, ga, mc, mct, twr, twi = _dft_tables(n1, kz, kzp)
    seq4 = lambda a: a.reshape(a.shape[0], kz, FFT_N2, D_HYENA)

    zf = _filter_features(l_true, lp)
    taps, sumsq = _hyena_filter(zf, l_true, cfg["tl"], p["hy_w1"], p["hy_b1"], p["hy_freq1"], p["hy_w2"],
                                p["hy_b2"], p["hy_freq2"], p["hy_w3"], p["hy_decay"])
    rs = lax.rsqrt(sumsq)
    kf = _dft_b_filter(mc, twr, twi, _dft_a(fa, seq4(taps)), rs)

    wb = _dft_b(mc, mct, twr, twi, _dft_a(fa, seq4(z)), kf)
    y_a = _dft_c(ga, wb, seq4(x0), seq4(z), p["hy_skip_d"]).reshape(rows, D_HYENA)

    h = _mm_res_ln([y_a, y_b.reshape(rows, D_CONF)], [p["ev_w_out_a"], p["ev_w_out_b"]], p["ev_b_out"],
                   h, p["ln1_g"][0], p["ln1_b"][0], cfg["tm"], "out_proj_even")
    h = _mlp(h, p["mlp_w1"][0], p["mlp_w2"][0], p["ln2_g"][0], p["ln2_b"][0], cfg["tmlp"], 1024)

    cos_t, sin_t = _rope_slot_tables(lp)
    q, k, v = _mla_proj(h.reshape(bsz, lp, D_MODEL), l_true, cfg["tl"], cos_t, sin_t,
                        p["mla_wq_a"], p["mla_q_norm"], p["wq1"], p["wq2"], p["wc"], p["mla_kv_norm"],
                        p["wk1"], p["wk2"], p["wkk"], p["wkv"])
    o = _attention(q, k, v, cfg["tq"], cfg["tk"], cfg["tr"]).reshape(rows, N_HEADS * HEAD_SLOT)
    h = _mm_res_ln([o], [p["wo_slot"]], jnp.zeros((1, D_MODEL), F32), h, p["ln1_g"][1], p["ln1_b"][1],
                   cfg["tm"], "out_proj_odd")
    h = _mlp(h, p["mlp_w1"][1], p["mlp_w2"][1], p["ln2_g"][1], p["ln2_b"][1], cfg["tmlp"], 1024)

    return h.reshape(bsz, lp, D_MODEL)[:, N_META:l_true]


def kernel(x_prompt, x_sample, meta_tokens, ev_w_in, ev_b_in, ev_short_w, ev_short_b, hy_w1, hy_b1, hy_freq1,
           hy_w2, hy_b2, hy_freq2, hy_w3, hy_decay, hy_skip_d, cf_dw_w, cf_dw_b, cf_ln_g, cf_ln_b, ev_w_out,
           ev_b_out, mla_wq_a, mla_q_norm, mla_wq_b, mla_wkv_a, mla_kv_norm, mla_wkv_b, mla_wo, ln1_g, ln1_b,
           mlp_w1, mlp_w2, ln2_g, ln2_b):
    row = lambda a: a.reshape(1, -1)
    wq1, wq2, wc, wk1, wk2, wkk, wkv, wo_slot = _mla_weights(mla_wq_b[0], mla_wkv_a[0], mla_wkv_b[0], mla_wo[0])
    p = dict(
        meta_tokens=meta_tokens,
        ev_w_in=ev_w_in[0].astype(BF16), ev_b_in=row(ev_b_in[0]),
        ev_short_w=ev_short_w[0], ev_short_b=row(ev_short_b[0]),
        hy_w1=jnp.pad(hy_w1[0], ((0, 0), (0, FILTER_EMB_PAD - FILTER_EMB), (0, 0))),
        hy_b1=hy_b1[0][:, None, :], hy_freq1=hy_freq1[0][:, None, :],
        hy_w2=hy_w2[0], hy_b2=hy_b2[0][:, None, :], hy_freq2=hy_freq2[0][:, None, :],
        hy_w3=hy_w3[0], hy_decay=hy_decay[0][:, None, :], hy_skip_d=row(hy_skip_d[0]),
        cf_dw_w=cf_dw_w[0], cf_dw_b=row(cf_dw_b[0]), cf_ln_g=row(cf_ln_g[0]), cf_ln_b=row(cf_ln_b[0]),
        ev_w_out_a=ev_w_out[0, :D_HYENA].astype(BF16), ev_w_out_b=ev_w_out[0, D_HYENA:].astype(BF16),
        ev_b_out=row(ev_b_out[0]),
        mla_wq_a=mla_wq_a[0].astype(BF16), mla_q_norm=row(mla_q_norm[0]), mla_kv_norm=row(mla_kv_norm[0]),
        wq1=wq1, wq2=wq2, wc=wc, wk1=wk1, wk2=wk2, wkk=wkk, wkv=wkv, wo_slot=wo_slot,
        ln1_g=ln1_g[:, None, :], ln1_b=ln1_b[:, None, :], ln2_g=ln2_g[:, None, :], ln2_b=ln2_b[:, None, :],
        mlp_w1=mlp_w1.astype(BF16), mlp_w2=mlp_w2.astype(BF16),
    )
    return (_trunk(x_prompt, p), _trunk(x_sample, p))
```

```python
import functools
import math

import jax
import jax.numpy as jnp
import numpy as np
from jax import lax
from jax.experimental import pallas as pl
from jax.experimental.pallas import tpu as pltpu

F32 = jnp.float32
BF16 = jnp.bfloat16

D_MODEL = 1024
N_META = 16
D_HYENA = 512
D_CONF = 512
D_IN_EVEN = 3 * D_HYENA + 2 * D_CONF
CONF_K = 31
FILTER_EMB = 33
FILTER_EMB_PAD = 40
FILTER_HIDDEN = 64
N_HEADS = 16
QK_NOPE = 64
QK_ROPE = 32
V_HEAD = 64
Q_RANK = 384
KV_RANK = 256
ROPE_THETA = 10000.0
D_FF = 4096
DEPTH = 2
DN_ALPHA = (2 * DEPTH) ** 0.25
LN_EPS = 1e-5
RMS_EPS = 1e-6

LANES = 128
SUBLANES = 8
HEAD_SLOT = LANES
HALO = 16
VMEM_LIMIT = 56 * 1024 * 1024

PAD_LANE = QK_NOPE + QK_ROPE
SUM_LANE = V_HEAD
PAD_SCORE = -1e30
FFT_N2 = 128

HIGHEST = lax.Precision.HIGHEST


def _params(sem, vmem=VMEM_LIMIT):
    return pltpu.CompilerParams(dimension_semantics=sem, vmem_limit_bytes=vmem)


def _round_up(x, m):
    return -(-x // m) * m


def _layer_norm(x, g, b):
    mu = jnp.mean(x, axis=-1, keepdims=True)
    xc = x - mu
    var = jnp.mean(xc * xc, axis=-1, keepdims=True)
    return xc * lax.rsqrt(var + LN_EPS) * g + b


def _mm_bias_kernel(x_ref, w_ref, b_ref, o_ref):
    acc = jnp.dot(x_ref[...].astype(BF16), w_ref[...], preferred_element_type=F32)
    o_ref[...] = (acc + b_ref[...]).astype(o_ref.dtype)


def _mm_bias(x, w, b, tm, out_dtype=F32):
    rows, k = x.shape
    n = w.shape[1]
    return pl.pallas_call(
        _mm_bias_kernel,
        grid=(pl.cdiv(rows, tm),),
        in_specs=[
            pl.BlockSpec((tm, k), lambda i: (i, 0)),
            pl.BlockSpec((k, n), lambda i: (0, 0)),
            pl.BlockSpec((1, n), lambda i: (0, 0)),
        ],
        out_specs=pl.BlockSpec((tm, n), lambda i: (i, 0)),
        out_shape=jax.ShapeDtypeStruct((rows, n), out_dtype),
        compiler_params=_params(("parallel",)),
        name="in_proj",
    )(x, w, b)


def _mm_res_ln_kernel(n_in, *refs):
    a_refs = refs[:n_in]
    w_refs = refs[n_in:2 * n_in]
    bias_ref, res_ref, g_ref, b_ref, o_ref = refs[2 * n_in:]
    acc = bias_ref[...] + DN_ALPHA * res_ref[...]
    for a_ref, w_ref in zip(a_refs, w_refs):
        acc = acc + jnp.dot(a_ref[...].astype(BF16), w_ref[...], preferred_element_type=F32)
    o_ref[...] = _layer_norm(acc, g_ref[...], b_ref[...])


def _mm_res_ln(a_list, w_list, bias, res, g, b, tm, name):
    rows = res.shape[0]
    n_in = len(a_list)
    in_specs = [pl.BlockSpec((tm, a.shape[1]), lambda i: (i, 0)) for a in a_list]
    in_specs += [pl.BlockSpec(w.shape, lambda i: (0, 0)) for w in w_list]
    in_specs += [
        pl.BlockSpec((1, D_MODEL), lambda i: (0, 0)),
        pl.BlockSpec((tm, D_MODEL), lambda i: (i, 0)),
        pl.BlockSpec((1, D_MODEL), lambda i: (0, 0)),
        pl.BlockSpec((1, D_MODEL), lambda i: (0, 0)),
    ]
    return pl.pallas_call(
        functools.partial(_mm_res_ln_kernel, n_in),
        grid=(pl.cdiv(rows, tm),),
        in_specs=in_specs,
        out_specs=pl.BlockSpec((tm, D_MODEL), lambda i: (i, 0)),
        out_shape=jax.ShapeDtypeStruct((rows, D_MODEL), F32),
        compiler_params=_params(("parallel",)),
        name=name,
    )(*a_list, *w_list, bias, res, g, b)


MLP_HIDDEN_CHUNK = 1024


def _mlp_kernel(x_ref, w1_ref, w2_ref, g_ref, b_ref, o_ref):
    x = x_ref[...]
    xb = x.astype(BF16)
    acc = DN_ALPHA * x
    for c0 in range(0, D_FF, MLP_HIDDEN_CHUNK):
        hid = jnp.dot(xb, w1_ref[:, c0:c0 + MLP_HIDDEN_CHUNK], preferred_element_type=F32)
        hid = jnp.maximum(hid, 0.0)
        hid = (hid * hid).astype(BF16)
        acc = acc + jnp.dot(hid, w2_ref[c0:c0 + MLP_HIDDEN_CHUNK, :], preferred_element_type=F32)
    o_ref[...] = _layer_norm(acc, g_ref[...], b_ref[...])


def _mlp(x, w1, w2, g, b, tm):
    rows = x.shape[0]
    return pl.pallas_call(
        _mlp_kernel,
        grid=(pl.cdiv(rows, tm),),
        in_specs=[
            pl.BlockSpec((tm, D_MODEL), lambda i: (i, 0)),
            pl.BlockSpec((D_MODEL, D_FF), lambda i: (0, 0)),
            pl.BlockSpec((D_FF, D_MODEL), lambda i: (0, 0)),
            pl.BlockSpec((1, D_MODEL), lambda i: (0, 0)),
            pl.BlockSpec((1, D_MODEL), lambda i: (0, 0)),
        ],
        out_specs=pl.BlockSpec((tm, D_MODEL), lambda i: (i, 0)),
        out_shape=jax.ShapeDtypeStruct((rows, D_MODEL), F32),
        compiler_params=_params(("parallel",)),
        name="mlp",
    )(x, w1, w2, g, b)


def _conv_mix_kernel(l_true, tl, rc,
                     main_ref, prev_ref, next_ref, sw_ref, sb_ref, dw_ref, db_ref, g_ref, b_ref,
                     x0_ref, z_ref, yb_ref, ext_ref, glu_ref):
    i = pl.program_id(1)
    row0 = i * tl - HALO

    def valid_rows(start, n):
        g = start + lax.broadcasted_iota(jnp.int32, (n, 1), 0)
        return (g >= 0) & (g < l_true)

    ext_ref[0:HALO, :] = jnp.where(valid_rows(row0, HALO), prev_ref[0], 0.0)
    ext_ref[HALO:HALO + tl, :] = jnp.where(valid_rows(row0 + HALO, tl), main_ref[0], 0.0)
    ext_ref[HALO + tl:, :] = jnp.where(valid_rows(row0 + HALO + tl, HALO), next_ref[0], 0.0)

    a = ext_ref[:, 3 * D_HYENA:3 * D_HYENA + D_CONF]
    gate = ext_ref[:, 3 * D_HYENA + D_CONF:]
    glu_ref[...] = a * jax.nn.sigmoid(gate)

    sw = sw_ref[...]
    sb = sb_ref[...]
    dw = dw_ref[...]
    half = (CONF_K - 1) // 2

    def chunk(c, carry):
        r0 = pl.multiple_of(c * rc, SUBLANES)
        rows_ok = valid_rows(i * tl + r0, rc)

        win = ext_ref[pl.ds(r0 + HALO - SUBLANES, rc + 2 * SUBLANES), 0:3 * D_HYENA]
        hy = sb
        for k in range(3):
            off = SUBLANES + k - 1
            hy = hy + sw[k:k + 1, :] * win[off:off + rc, :]
        x0_ref[0, pl.ds(r0, rc), :] = hy[:, 0:D_HYENA]
        zz = hy[:, D_HYENA:2 * D_HYENA] * hy[:, 2 * D_HYENA:]
        z_ref[0, pl.ds(r0, rc), :] = jnp.where(rows_ok, zz, 0.0)

        gwin = glu_ref[pl.ds(r0, rc + 2 * HALO), :]
        acc = db_ref[...]
        for j in range(SUBLANES):
            part = None
            for k in range(CONF_K):
                off = HALO + k - half
                if off % SUBLANES == j:
                    base = off - j
                    term = dw[k:k + 1, :] * gwin[base:base + rc + SUBLANES, :]
                    part = term if part is None else part + term
            acc = acc + part[j:j + rc, :]
        y = _layer_norm(acc, g_ref[...], b_ref[...])
        yb_ref[0, pl.ds(r0, rc), :] = y * jax.nn.sigmoid(y)
        return carry

    lax.fori_loop(0, tl // rc, chunk, 0)


def _conv_mix(proj, l_true, tl, short_w, short_b, dw_w, dw_b, ln_g, ln_b):
    bsz, lp, _ = proj.shape
    nt = lp // tl
    hb = tl // HALO
    rc = _divisor_tile(tl, 64, SUBLANES)
    out_sds = jax.ShapeDtypeStruct((bsz, lp, D_HYENA), F32)
    row_spec = lambda shape: pl.BlockSpec(shape, lambda b, i: (0, 0))
    return pl.pallas_call(
        functools.partial(_conv_mix_kernel, l_true, tl, rc),
        grid=(bsz, nt),
        in_specs=[
            pl.BlockSpec((1, tl, D_IN_EVEN), lambda b, i: (b, i, 0)),
            pl.BlockSpec((1, HALO, D_IN_EVEN), lambda b, i: (b, jnp.maximum(i * hb - 1, 0), 0)),
            pl.BlockSpec((1, HALO, D_IN_EVEN), lambda b, i: (b, jnp.minimum((i + 1) * hb, nt * hb - 1), 0)),
            row_spec((3, 3 * D_HYENA)),
            row_spec((1, 3 * D_HYENA)),
            row_spec((CONF_K, D_CONF)),
            row_spec((1, D_CONF)),
            row_spec((1, D_CONF)),
            row_spec((1, D_CONF)),
        ],
        out_specs=[pl.BlockSpec((1, tl, D_HYENA), lambda b, i: (b, i, 0))] * 3,
        out_shape=[out_sds, out_sds, out_sds],
        scratch_shapes=[
            pltpu.VMEM((tl + 2 * HALO, D_IN_EVEN), F32),
            pltpu.VMEM((tl + 2 * HALO, D_CONF), F32),
        ],
        compiler_params=_params(("parallel", "parallel")),
        name="conv_mix",
    )(proj, proj, proj, short_w, short_b, dw_w, dw_b, ln_g, ln_b)


def _filter_kernel(l_true, tl, zf_ref, w1_ref, b1_ref, f1_ref, w2_ref, b2_ref, f2_ref, w3_ref, dec_ref,
                   h_ref, ss_ref):
    i = pl.program_id(0)

    @pl.when(i == 0)
    def _():
        ss_ref[...] = jnp.zeros_like(ss_ref)

    zf = zf_ref[...]
    t = zf[:, 0:1]
    g = i * tl + lax.broadcasted_iota(jnp.int32, (tl, 1), 0)
    total = jnp.zeros((1, D_HYENA), F32)
    for d in range(2):
        a = jnp.dot(zf, w1_ref[d], precision=HIGHEST, preferred_element_type=F32) + b1_ref[d]
        a = jnp.sin(f1_ref[d] * a)
        a = jnp.dot(a, w2_ref[d], precision=HIGHEST, preferred_element_type=F32) + b2_ref[d]
        a = jnp.sin(f2_ref[d] * a)
        hh = jnp.dot(a, w3_ref[d], precision=HIGHEST, preferred_element_type=F32)
        hh = hh * jnp.exp(-t * dec_ref[d])
        keep = (g < l_true) & (g >= d)
        hh = jnp.where(keep, hh, 0.0)
        h_ref[d] = hh
        total = total + jnp.sum(hh * hh, axis=0, keepdims=True)
    ss_ref[...] += total


def _hyena_filter(zf, l_true, tl, w1, b1, f1, w2, b2, f2, w3, decay):
    lp = zf.shape[0]
    full = lambda shape: pl.BlockSpec(shape, lambda i: (0,) * len(shape))
    return pl.pallas_call(
        functools.partial(_filter_kernel, l_true, tl),
        grid=(lp // tl,),
        in_specs=[
            pl.BlockSpec((tl, FILTER_EMB_PAD), lambda i: (i, 0)),
            full((2, FILTER_EMB_PAD, FILTER_HIDDEN)),
            full((2, 1, FILTER_HIDDEN)),
            full((2, 1, FILTER_HIDDEN)),
            full((2, FILTER_HIDDEN, FILTER_HIDDEN)),
            full((2, 1, FILTER_HIDDEN)),
            full((2, 1, FILTER_HIDDEN)),
            full((2, FILTER_HIDDEN, D_HYENA)),
            full((2, 1, D_HYENA)),
        ],
        out_specs=[
            pl.BlockSpec((2, tl, D_HYENA), lambda i: (0, i, 0)),
            pl.BlockSpec((1, D_HYENA), lambda i: (0, 0)),
        ],
        out_shape=[
            jax.ShapeDtypeStruct((2, lp, D_HYENA), F32),
            jax.ShapeDtypeStruct((1, D_HYENA), F32),
        ],
        compiler_params=_params(("arbitrary",)),
        name="hyena_filter",
    )(zf, w1, b1, f1, w2, b2, f2, w3, decay)


S2_BLOCK = SUBLANES
BF16_ROWS = 2 * SUBLANES


def _split_bf16(x):
    hi = x.astype(BF16)
    return hi, (x - hi.astype(F32)).astype(BF16)


def _dot3(m_ref, x):
    x_hi, x_lo = _split_bf16(x)
    dot = functools.partial(jnp.dot, preferred_element_type=F32)
    return dot(m_ref[0], x_hi) + (dot(m_ref[0], x_lo) + dot(m_ref[1], x_hi))


def _dft_a_kernel(kz, kzp, f_ref, x_ref, o_ref):
    for j in range(S2_BLOCK):
        x = x_ref[0, :, j, :]
        if kzp != kz:
            x = jnp.concatenate([x, jnp.zeros((kzp - kz, x.shape[1]), F32)], axis=0)
        o_ref[0, :, j, :] = _dot3(f_ref, x)


def _dft_a(fmat, x):
    bsz, kz, n2, ch = x.shape
    _, m, kzp = fmat.shape
    return pl.pallas_call(
        functools.partial(_dft_a_kernel, kz, kzp),
        grid=(bsz, n2 // S2_BLOCK),
        in_specs=[
            pl.BlockSpec((2, m, kzp), lambda b, j: (0, 0, 0)),
            pl.BlockSpec((1, kz, S2_BLOCK, ch), lambda b, j: (b, 0, j, 0)),
        ],
        out_specs=pl.BlockSpec((1, m, S2_BLOCK, ch), lambda b, j: (b, 0, j, 0)),
        out_shape=jax.ShapeDtypeStruct((bsz, m, n2, ch), F32),
        compiler_params=_params(("parallel", "parallel")),
        name="dft_stage_a",
    )(fmat, x)


def _twiddle(ar, ai, twr, twi):
    return ar * twr - ai * twi, ar * twi + ai * twr


def _dft_b_filter_kernel(m_ref, twr_ref, twi_ref, a_ref, rs_ref, kf_ref):
    twr, twi = twr_ref[0], twi_ref[0]
    fwd = jnp.concatenate(_twiddle(a_ref[0, 0, 0], a_ref[0, 1, 0], twr, twi), axis=0)
    bwd = jnp.concatenate(_twiddle(a_ref[1, 0, 0], a_ref[1, 1, 0], twr, twi), axis=0)
    zf = _dot3(m_ref, fwd)
    zb = _dot3(m_ref, bwd)
    rs = rs_ref[...]
    kf_ref[0, 0] = (zf[:FFT_N2] + zb[:FFT_N2]) * rs
    kf_ref[1, 0] = (zf[FFT_N2:] - zb[FFT_N2:]) * rs


def _dft_b_filter(mc, twr, twi, a, rs):
    n1 = twr.shape[0]
    a5 = a.reshape(2, 2, n1, FFT_N2, D_HYENA)
    tw_spec = pl.BlockSpec((1, FFT_N2, 1), lambda f: (f, 0, 0))
    return pl.pallas_call(
        _dft_b_filter_kernel,
        grid=(n1,),
        in_specs=[
            pl.BlockSpec((2, 2 * FFT_N2, 2 * FFT_N2), lambda f: (0, 0, 0)),
            tw_spec, tw_spec,
            pl.BlockSpec((2, 2, 1, FFT_N2, D_HYENA), lambda f: (0, 0, f, 0, 0)),
            pl.BlockSpec((1, D_HYENA), lambda f: (0, 0)),
        ],
        out_specs=pl.BlockSpec((2, 1, FFT_N2, D_HYENA), lambda f: (0, f, 0, 0)),
        out_shape=jax.ShapeDtypeStruct((2, n1, FFT_N2, D_HYENA), F32),
        compiler_params=_params(("parallel",)),
        name="dft_filter_spectrum",
    )(mc, twr, twi, a5, rs)


def _dft_b_kernel(m_ref, mt_ref, twr_ref, twi_ref, a_ref, kf_ref, o_ref):
    twr, twi = twr_ref[0], twi_ref[0]
    x = jnp.concatenate(_twiddle(a_ref[0, 0, 0], a_ref[0, 1, 0], twr, twi), axis=0)
    z = _dot3(m_ref, x)
    y = jnp.concatenate(_twiddle(z[:FFT_N2], z[FFT_N2:], kf_ref[0, 0], kf_ref[1, 0]), axis=0)
    w = _dot3(mt_ref, y)
    o_ref[0, 0, 0], o_ref[0, 1, 0] = _twiddle(w[:FFT_N2], w[FFT_N2:], twr, -twi)


def _dft_b(mc, mct, twr, twi, a, kf):
    bsz = a.shape[0]
    n1 = twr.shape[0]
    a5 = a.reshape(bsz, 2, n1, FFT_N2, D_HYENA)
    mat_spec = pl.BlockSpec((2, 2 * FFT_N2, 2 * FFT_N2), lambda b, f: (0, 0, 0))
    tw_spec = pl.BlockSpec((1, FFT_N2, 1), lambda b, f: (f, 0, 0))
    out = pl.pallas_call(
        _dft_b_kernel,
        grid=(bsz, n1),
        in_specs=[
            mat_spec, mat_spec, tw_spec, tw_spec,
            pl.BlockSpec((1, 2, 1, FFT_N2, D_HYENA), lambda b, f: (b, 0, f, 0, 0)),
            pl.BlockSpec((2, 1, FFT_N2, D_HYENA), lambda b, f: (0, f, 0, 0)),
        ],
        out_specs=pl.BlockSpec((1, 2, 1, FFT_N2, D_HYENA), lambda b, f: (b, 0, f, 0, 0)),
        out_shape=jax.ShapeDtypeStruct((bsz, 2, n1, FFT_N2, D_HYENA), F32),
        compiler_params=_params(("parallel", "parallel")),
        name="dft_stage_b",
    )(mc, mct, twr, twi, a5, kf)
    return out.reshape(bsz, 2 * n1, FFT_N2, D_HYENA)


def _dft_c_kernel(kz, g_ref, w_ref, x0_ref, z_ref, d_ref, o_ref):
    for j in range(S2_BLOCK):
        y = jnp.dot(g_ref[...], w_ref[0, :, j, :], precision=HIGHEST, preferred_element_type=F32)
        o_ref[0, :, j, :] = x0_ref[0, :, j, :] * (y[:kz] + z_ref[0, :, j, :] * d_ref[...])


def _dft_c(gmat, w, x0, z, skip_d):
    bsz, kz, n2, ch = x0.shape
    kzp, m = gmat.shape
    seq_spec = pl.BlockSpec((1, kz, S2_BLOCK, ch), lambda b, j: (b, 0, j, 0))
    return pl.pallas_call(
        functools.partial(_dft_c_kernel, kz),
        grid=(bsz, n2 // S2_BLOCK),
        in_specs=[
            pl.BlockSpec((kzp, m), lambda b, j: (0, 0)),
            pl.BlockSpec((1, m, S2_BLOCK, ch), lambda b, j: (b, 0, j, 0)),
            seq_spec, seq_spec,
            pl.BlockSpec((1, ch), lambda b, j: (0, 0)),
        ],
        out_specs=seq_spec,
        out_shape=jax.ShapeDtypeStruct((bsz, kz, n2, ch), F32),
        compiler_params=_params(("parallel", "parallel")),
        name="dft_stage_c",
    )(gmat, w, x0, z, skip_d)


def _dft_tables(n1, kz, kzp):
    n2 = FFT_N2
    n = n1 * n2
    f1 = np.arange(n1)
    s1 = np.arange(kzp)
    ang = (2.0 * np.pi / n1) * ((f1[:, None] * s1[None, :]) % n1)
    live = (s1 < kz)[None, :]
    fa = np.concatenate([np.where(live, np.cos(ang), 0.0), np.where(live, -np.sin(ang), 0.0)], axis=0)
    ga = np.concatenate([np.cos(ang).T, -np.sin(ang).T], axis=1) * np.where(s1 < kz, 1.0 / n, 0.0)[:, None]
    idx = np.arange(n2)
    ang2 = (2.0 * np.pi / n2) * ((idx[:, None] * idx[None, :]) % n2)
    cr, ci = np.cos(ang2), -np.sin(ang2)
    mc = np.block([[cr, -ci], [ci, cr]])
    mct = np.block([[cr, ci], [-ci, cr]])
    ang3 = (2.0 * np.pi / n) * ((f1[:, None] * idx[None, :]) % n)
    twr, twi = np.cos(ang3)[:, :, None], -np.sin(ang3)[:, :, None]

    def hi_lo(mat):
        hi = mat.astype(BF16)
        lo = (mat - hi.astype(np.float64)).astype(BF16)
        return jnp.asarray(np.stack([hi, lo]))

    return hi_lo(fa), jnp.asarray(ga, F32), hi_lo(mc), hi_lo(mct), jnp.asarray(twr, F32), jnp.asarray(twi, F32)


def _rms(x, g):
    return x * lax.rsqrt(jnp.mean(x * x, axis=-1, keepdims=True) + RMS_EPS) * g


def _mla_proj_kernel(l_true, tm,
                     x_ref, cos_ref, sin_ref, wqa_ref, qn_ref, wq1_ref, wq2_ref, wc_ref, kn_ref,
                     wk1_ref, wk2_ref, wkk_ref, wkv_ref, q_ref, k_ref, v_ref):
    i = pl.program_id(1)
    x = x_ref[0].astype(BF16)
    cos = cos_ref[...]
    sin = sin_ref[...]
    lane = lax.broadcasted_iota(jnp.int32, (1, HEAD_SLOT), 1)

    cq = _rms(jnp.dot(x, wqa_ref[...], preferred_element_type=F32), qn_ref[...]).astype(BF16)
    q1 = jnp.dot(cq, wq1_ref[...], preferred_element_type=F32)
    q2 = jnp.dot(cq, wq2_ref[...], preferred_element_type=F32)
    q_one = jnp.where(lane == PAD_LANE, 1.0, 0.0)

    ckv = _rms(jnp.dot(x, wc_ref[...], preferred_element_type=F32), kn_ref[...]).astype(BF16)
    kn = jnp.dot(ckv, wkk_ref[...], preferred_element_type=F32)
    vv = jnp.dot(ckv, wkv_ref[...], preferred_element_type=F32)
    k1 = jnp.dot(x, wk1_ref[...], preferred_element_type=F32)
    k2 = jnp.dot(x, wk2_ref[...], preferred_element_type=F32)
    rows = i * tm + lax.broadcasted_iota(jnp.int32, (tm, 1), 0)
    k_pe = k1 * cos + k2 * sin + jnp.where((rows >= l_true) & (lane == PAD_LANE), PAD_SCORE, 0.0)
    v_one = jnp.where(lane == SUM_LANE, 1.0, 0.0)

    for h in range(N_HEADS):
        sl = slice(h * HEAD_SLOT, (h + 1) * HEAD_SLOT)
        q_ref[0, :, sl] = (q1[:, sl] * cos + q2[:, sl] * sin + q_one).astype(BF16)
        k_ref[0, :, sl] = (kn[:, sl] + k_pe).astype(BF16)
        v_ref[0, :, sl] = (vv[:, sl] + v_one).astype(BF16)


def _mla_proj(h, l_true, tm, cos_t, sin_t, wqa, qn, wq1, wq2, wc, kn, wk1, wk2, wkk, wkv):
    bsz, lp, _ = h.shape
    wide = N_HEADS * HEAD_SLOT
    full = lambda a: pl.BlockSpec(a.shape, lambda b, i: (0,) * a.ndim)
    out_sds = jax.ShapeDtypeStruct((bsz, lp, wide), BF16)
    weights = (wqa, qn, wq1, wq2, wc, kn, wk1, wk2, wkk, wkv)
    return pl.pallas_call(
        functools.partial(_mla_proj_kernel, l_true, tm),
        grid=(bsz, lp // tm),
        in_specs=[
            pl.BlockSpec((1, tm, D_MODEL), lambda b, i: (b, i, 0)),
            pl.BlockSpec((tm, HEAD_SLOT), lambda b, i: (i, 0)),
            pl.BlockSpec((tm, HEAD_SLOT), lambda b, i: (i, 0)),
        ] + [full(w) for w in weights],
        out_specs=[pl.BlockSpec((1, tm, wide), lambda b, i: (b, i, 0))] * 3,
        out_shape=[out_sds, out_sds, out_sds],
        compiler_params=_params(("parallel", "parallel")),
        name="mla_proj",
    )(h, cos_t, sin_t, *weights)


def _attn_kernel(tk, tr, q_ref, k_ref, v_ref, o_ref, s0_ref, s1_ref, m_ref, acc_ref):
    tq = q_ref.shape[1]
    nk = k_ref.shape[1] // tk
    s_refs = (s0_ref, s1_ref)
    row_blocks = [slice(r * tr, (r + 1) * tr) for r in range(tq // tr)]

    def scores(c, slot):
        k = k_ref[0, pl.ds(pl.multiple_of(c * tk, tk), tk), :]
        for rows in row_blocks:
            s_refs[slot][rows, :] = lax.dot_general(q_ref[0, rows, :], k, (((1,), (1,)), ((), ())),
                                                    preferred_element_type=F32)

    def softmax_pv(c, slot):
        v = v_ref[0, pl.ds(pl.multiple_of(c * tk, tk), tk), :]
        for rows in row_blocks:
            s = s_refs[slot][rows, :]
            m_old = m_ref[rows, :]
            m_new = jnp.maximum(m_old, jnp.max(s, axis=-1, keepdims=True))
            p = jnp.exp2(s - m_new).astype(BF16)
            acc_ref[rows, :] = (jnp.exp2(m_old - m_new) * acc_ref[rows, :]
                                + jnp.dot(p, v, preferred_element_type=F32))
            m_ref[rows, :] = m_new

    def step(c, slot):
        scores(c + 1, 1 - slot)
        softmax_pv(c, slot)

    m_ref[...] = jnp.full_like(m_ref, -jnp.inf)
    acc_ref[...] = jnp.zeros_like(acc_ref)
    scores(0, 0)

    def pair(j, carry):
        step(2 * j, 0)
        step(2 * j + 1, 1)
        return carry

    lax.fori_loop(0, (nk - 1) // 2, pair, 0)
    if (nk - 1) % 2:
        step(nk - 2, (nk - 2) % 2)
    softmax_pv(nk - 1, (nk - 1) % 2)
    acc = acc_ref[...]
    o_ref[0] = (acc / acc[:, SUM_LANE:SUM_LANE + 1]).astype(o_ref.dtype)


def _attention(q, k, v, tq, tk, tr):
    bsz, lp, wide = q.shape
    return pl.pallas_call(
        functools.partial(_attn_kernel, tk, tr),
        grid=(bsz, N_HEADS, lp // tq),
        in_specs=[
            pl.BlockSpec((1, tq, HEAD_SLOT), lambda b, h, i: (b, i, h)),
            pl.BlockSpec((1, lp, HEAD_SLOT), lambda b, h, i: (b, 0, h)),
            pl.BlockSpec((1, lp, HEAD_SLOT), lambda b, h, i: (b, 0, h)),
        ],
        out_specs=pl.BlockSpec((1, tq, HEAD_SLOT), lambda b, h, i: (b, i, h)),
        out_shape=jax.ShapeDtypeStruct((bsz, lp, wide), BF16),
        scratch_shapes=[pltpu.VMEM((tq, tk), F32), pltpu.VMEM((tq, tk), F32),
                        pltpu.VMEM((tq, 1), F32), pltpu.VMEM((tq, HEAD_SLOT), F32)],
        compiler_params=_params(("parallel", "parallel", "arbitrary")),
        name="attention",
    )(q, k, v)


def _slot_cols(w, per_head, pieces):
    k = w.shape[0]
    wh = w.reshape(k, N_HEADS, per_head)
    out = jnp.zeros((k, N_HEADS, HEAD_SLOT), w.dtype)
    for dst, src, width, sign in pieces:
        out = out.at[:, :, dst:dst + width].set(sign * wh[:, :, src:src + width])
    return out.reshape(k, N_HEADS * HEAD_SLOT)


def _mla_weights(wq_b, wkv_a, wkv_b, wo):
    half = QK_ROPE // 2
    qk = QK_NOPE + QK_ROPE
    q_scale = (qk ** -0.5) * math.log2(math.e)
    wq1 = _slot_cols(wq_b, qk, [(0, 0, QK_NOPE, 1.0), (QK_NOPE, QK_NOPE, QK_ROPE, 1.0)]) * q_scale
    wq2 = _slot_cols(wq_b, qk, [(QK_NOPE, QK_NOPE + half, half, -1.0),
                                (QK_NOPE + half, QK_NOPE, half, 1.0)]) * q_scale
    wc = wkv_a[:, :KV_RANK]
    pe = wkv_a[:, KV_RANK:]
    zeros = lambda n: jnp.zeros((D_MODEL, n), wkv_a.dtype)
    tail = HEAD_SLOT - QK_NOPE - QK_ROPE
    wk1 = jnp.concatenate([zeros(QK_NOPE), pe, zeros(tail)], axis=1)
    wk2 = jnp.concatenate([zeros(QK_NOPE), -pe[:, half:], pe[:, :half], zeros(tail)], axis=1)
    wkk = _slot_cols(wkv_b, QK_NOPE + V_HEAD, [(0, 0, QK_NOPE, 1.0)])
    wkv = _slot_cols(wkv_b, QK_NOPE + V_HEAD, [(0, QK_NOPE, V_HEAD, 1.0)])
    wo_slot = jnp.zeros((N_HEADS, HEAD_SLOT, D_MODEL), wo.dtype)
    wo_slot = wo_slot.at[:, :V_HEAD, :].set(wo.reshape(N_HEADS, V_HEAD, D_MODEL))
    wo_slot = wo_slot.reshape(N_HEADS * HEAD_SLOT, D_MODEL)
    cast = lambda a: a.astype(BF16)
    return tuple(map(cast, (wq1, wq2, wc, wk1, wk2, wkk, wkv, wo_slot)))


def _rope_slot_tables(lp):
    pos = jnp.arange(lp, dtype=F32)
    inv = 1.0 / (ROPE_THETA ** (jnp.arange(0, QK_ROPE, 2, dtype=F32) / QK_ROPE))
    ang = pos[:, None] * inv[None, :]
    cos, sin = jnp.cos(ang), jnp.sin(ang)
    tail = HEAD_SLOT - QK_NOPE - QK_ROPE
    cos_t = jnp.concatenate([jnp.ones((lp, QK_NOPE), F32), cos, cos, jnp.zeros((lp, tail), F32)], axis=1)
    sin_t = jnp.concatenate([jnp.zeros((lp, QK_NOPE), F32), sin, sin, jnp.zeros((lp, tail), F32)], axis=1)
    return cos_t, sin_t


def _filter_features(l_true, lp):
    t = jnp.arange(lp, dtype=F32) / max(l_true - 1, 1)
    bands = (FILTER_EMB - 1) // 2
    freqs = jnp.linspace(1e-4, bands - 1, bands, dtype=F32)
    w = 2.0 * math.pi * jnp.arange(lp, dtype=F32) / l_true
    ang = w[:, None] * freqs[None, :]
    z = jnp.concatenate([t[:, None], jnp.cos(ang), -jnp.sin(ang)], axis=-1)
    return jnp.pad(z, ((0, 0), (0, FILTER_EMB_PAD - FILTER_EMB)))


def _divisor_tile(total, target, multiple):
    best = None
    for d in range(multiple, min(total, target) + 1, multiple):
        if total % d == 0:
            best = d
    assert best is not None, (total, target, multiple)
    return best


SHORT_SEQ_MAX = 2304
MXU_COLS = 256


def _tiling(bsz, l_true):
    lp = _round_up(l_true, LANES if l_true <= SHORT_SEQ_MAX else MXU_COLS)
    tk = _divisor_tile(lp, 1280, LANES // 2)
    tq = _divisor_tile(lp, 1280, 2 * SUBLANES)
    return dict(
        lp=lp, tk=tk,
        tl=_divisor_tile(lp, 640, HALO),
        tq=tq, tr=_divisor_tile(tq, 256, 2 * SUBLANES),
        tm=_divisor_tile(bsz * lp, 1088, 2 * SUBLANES),
        tmlp=_divisor_tile(bsz * lp, 640, 2 * SUBLANES),
    )


def _trunk(x, p):
    bsz, seq, _ = x.shape
    l_true = seq + N_META
    cfg = _tiling(bsz, l_true)
    lp = cfg["lp"]
    rows = bsz * lp

    meta = jnp.broadcast_to(p["meta_tokens"][None], (bsz, N_META, D_MODEL))
    h = jnp.concatenate([meta, x, jnp.zeros((bsz, lp - l_true, D_MODEL), F32)], axis=1)
    h = h.reshape(rows, D_MODEL)

    proj = _mm_bias(h, p["ev_w_in"], p["ev_b_in"], cfg["tm"]).reshape(bsz, lp, D_IN_EVEN)
    x0, z, y_b = _conv_mix(proj, l_true, cfg["tl"], p["ev_short_w"], p["ev_short_b"],
                           p["cf_dw_w"], p["cf_dw_b"], p["cf_ln_g"], p["cf_ln_b"])

    kz = lp // FFT_N2
    kzp = _round_up(kz, BF16_ROWS)
    n1 = _round_up(-(-(2 * l_true - 1) // FFT_N2), SUBLANES)
    fa, ga, mc, mct, twr, twi = _dft_tables(n1, kz, kzp)
    seq4 = lambda a: a.reshape(a.shape[0], kz, FFT_N2, D_HYENA)

    zf = _filter_features(l_true, lp)
    taps, sumsq = _hyena_filter(zf, l_true, cfg["tl"], p["hy_w1"], p["hy_b1"], p["hy_freq1"], p["hy_w2"],
                                p["hy_b2"], p["hy_freq2"], p["hy_w3"], p["hy_decay"])
    rs = lax.rsqrt(sumsq)
    kf = _dft_b_filter(mc, twr, twi, _dft_a(fa, seq4(taps)), rs)

    wb = _dft_b(mc, mct, twr, twi, _dft_a(fa, seq4(z)), kf)
    y_a = _dft_c(ga, wb, seq4(x0), seq4(z), p["hy_skip_d"]).reshape(rows, D_HYENA)

    h = _mm_res_ln([y_a, y_b.reshape(rows, D_CONF)], [p["ev_w_out_a"], p["ev_w_out_b"]], p["ev_b_out"],
                   h, p["ln1_g"][0], p["ln1_b"][0], cfg["tm"], "out_proj_even")
    h = _mlp(h, p["mlp_w1"][0], p["mlp_w2"][0], p["ln2_g"][0], p["ln2_b"][0], cfg["tmlp"])

    cos_t, sin_t = _rope_slot_tables(lp)
    q, k, v = _mla_proj(h.reshape(bsz, lp, D_MODEL), l_true, cfg["tl"], cos_t, sin_t,
                        p["mla_wq_a"], p["mla_q_norm"], p["wq1"], p["wq2"], p["wc"], p["mla_kv_norm"],
                        p["wk1"], p["wk2"], p["wkk"], p["wkv"])
    o = _attention(q, k, v, cfg["tq"], cfg["tk"], cfg["tr"]).reshape(rows, N_HEADS * HEAD_SLOT)
    h = _mm_res_ln([o], [p["wo_slot"]], jnp.zeros((1, D_MODEL), F32), h, p["ln1_g"][1], p["ln1_b"][1],
                   cfg["tm"], "out_proj_odd")
    h = _mlp(h, p["mlp_w1"][1], p["mlp_w2"][1], p["ln2_g"][1], p["ln2_b"][1], cfg["tmlp"])

    return h.reshape(bsz, lp, D_MODEL)[:, N_META:l_true]


def kernel(x_prompt, x_sample, meta_tokens, ev_w_in, ev_b_in, ev_short_w, ev_short_b, hy_w1, hy_b1, hy_freq1,
           hy_w2, hy_b2, hy_freq2, hy_w3, hy_decay, hy_skip_d, cf_dw_w, cf_dw_b, cf_ln_g, cf_ln_b, ev_w_out,
           ev_b_out, mla_wq_a, mla_q_norm, mla_wq_b, mla_wkv_a, mla_kv_norm, mla_wkv_b, mla_wo, ln1_g, ln1_b,
           mlp_w1, mlp_w2, ln2_g, ln2_b):
    row = lambda a: a.reshape(1, -1)
    wq1, wq2, wc, wk1, wk2, wkk, wkv, wo_slot = _mla_weights(mla_wq_b[0], mla_wkv_a[0], mla_wkv_b[0], mla_wo[0])
    p = dict(
        meta_tokens=meta_tokens,
        ev_w_in=ev_w_in[0].astype(BF16), ev_b_in=row(ev_b_in[0]),
        ev_short_w=ev_short_w[0], ev_short_b=row(ev_short_b[0]),
        hy_w1=jnp.pad(hy_w1[0], ((0, 0), (0, FILTER_EMB_PAD - FILTER_EMB), (0, 0))),
        hy_b1=hy_b1[0][:, None, :], hy_freq1=hy_freq1[0][:, None, :],
        hy_w2=hy_w2[0], hy_b2=hy_b2[0][:, None, :], hy_freq2=hy_freq2[0][:, None, :],
        hy_w3=hy_w3[0], hy_decay=hy_decay[0][:, None, :], hy_skip_d=row(hy_skip_d[0]),
        cf_dw_w=cf_dw_w[0], cf_dw_b=row(cf_dw_b[0]), cf_ln_g=row(cf_ln_g[0]), cf_ln_b=row(cf_ln_b[0]),
        ev_w_out_a=ev_w_out[0, :D_HYENA].astype(BF16), ev_w_out_b=ev_w_out[0, D_HYENA:].astype(BF16),
        ev_b_out=row(ev_b_out[0]),
        mla_wq_a=mla_wq_a[0].astype(BF16), mla_q_norm=row(mla_q_norm[0]), mla_kv_norm=row(mla_kv_norm[0]),
        wq1=wq1, wq2=wq2, wc=wc, wk1=wk1, wk2=wk2, wkk=wkk, wkv=wkv, wo_slot=wo_slot,
        ln1_g=ln1_g[:, None, :], ln1_b=ln1_b[:, None, :], ln2_g=ln2_g[:, None, :], ln2_b=ln2_b[:, None, :],
        mlp_w1=mlp_w1.astype(BF16), mlp_w2=mlp_w2.astype(BF16),
    )
    return (_trunk(x_prompt, p), _trunk(x_sample, p))
```

```python
import functools
import math

import jax
import jax.numpy as jnp
import numpy as np
from jax import lax
from jax.experimental import pallas as pl
from jax.experimental.pallas import tpu as pltpu

F32 = jnp.float32
BF16 = jnp.bfloat16

D_MODEL = 1024
N_META = 16
D_HYENA = 512
D_CONF = 512
D_IN_EVEN = 3 * D_HYENA + 2 * D_CONF
CONF_K = 31
FILTER_EMB = 33
FILTER_EMB_PAD = 40
FILTER_HIDDEN = 64
N_HEADS = 16
QK_NOPE = 64
QK_ROPE = 32
V_HEAD = 64
Q_RANK = 384
KV_RANK = 256
ROPE_THETA = 10000.0
D_FF = 4096
DEPTH = 2
DN_ALPHA = (2 * DEPTH) ** 0.25
LN_EPS = 1e-5
RMS_EPS = 1e-6

LANES = 128
SUBLANES = 8
HEAD_SLOT = LANES
HALO = 16
VMEM_LIMIT = 56 * 1024 * 1024

PAD_LANE = QK_NOPE + QK_ROPE
SUM_LANE = V_HEAD
PAD_SCORE = -1e30
FFT_N2 = 128

HIGHEST = lax.Precision.HIGHEST


def _params(sem, vmem=VMEM_LIMIT):
    return pltpu.CompilerParams(dimension_semantics=sem, vmem_limit_bytes=vmem)


def _round_up(x, m):
    return -(-x // m) * m


def _layer_norm(x, g, b):
    mu = jnp.mean(x, axis=-1, keepdims=True)
    xc = x - mu
    var = jnp.mean(xc * xc, axis=-1, keepdims=True)
    return xc * lax.rsqrt(var + LN_EPS) * g + b


def _mm_bias_kernel(x_ref, w_ref, b_ref, o_ref):
    acc = jnp.dot(x_ref[...].astype(BF16), w_ref[...], preferred_element_type=F32)
    o_ref[...] = (acc + b_ref[...]).astype(o_ref.dtype)


def _mm_bias(x, w, b, tm, out_dtype=F32):
    rows, k = x.shape
    n = w.shape[1]
    return pl.pallas_call(
        _mm_bias_kernel,
        grid=(pl.cdiv(rows, tm),),
        in_specs=[
            pl.BlockSpec((tm, k), lambda i: (i, 0)),
            pl.BlockSpec((k, n), lambda i: (0, 0)),
            pl.BlockSpec((1, n), lambda i: (0, 0)),
        ],
        out_specs=pl.BlockSpec((tm, n), lambda i: (i, 0)),
        out_shape=jax.ShapeDtypeStruct((rows, n), out_dtype),
        compiler_params=_params(("parallel",)),
        name="in_proj",
    )(x, w, b)


def _mm_res_ln_kernel(n_in, *refs):
    a_refs = refs[:n_in]
    w_refs = refs[n_in:2 * n_in]
    bias_ref, res_ref, g_ref, b_ref, o_ref = refs[2 * n_in:]
    acc = bias_ref[...] + DN_ALPHA * res_ref[...]
    for a_ref, w_ref in zip(a_refs, w_refs):
        acc = acc + jnp.dot(a_ref[...].astype(BF16), w_ref[...], preferred_element_type=F32)
    o_ref[...] = _layer_norm(acc, g_ref[...], b_ref[...])


def _mm_res_ln(a_list, w_list, bias, res, g, b, tm, name):
    rows = res.shape[0]
    n_in = len(a_list)
    in_specs = [pl.BlockSpec((tm, a.shape[1]), lambda i: (i, 0)) for a in a_list]
    in_specs += [pl.BlockSpec(w.shape, lambda i: (0, 0)) for w in w_list]
    in_specs += [
        pl.BlockSpec((1, D_MODEL), lambda i: (0, 0)),
        pl.BlockSpec((tm, D_MODEL), lambda i: (i, 0)),
        pl.BlockSpec((1, D_MODEL), lambda i: (0, 0)),
        pl.BlockSpec((1, D_MODEL), lambda i: (0, 0)),
    ]
    return pl.pallas_call(
        functools.partial(_mm_res_ln_kernel, n_in),
        grid=(pl.cdiv(rows, tm),),
        in_specs=in_specs,
        out_specs=pl.BlockSpec((tm, D_MODEL), lambda i: (i, 0)),
        out_shape=jax.ShapeDtypeStruct((rows, D_MODEL), F32),
        compiler_params=_params(("parallel",)),
        name=name,
    )(*a_list, *w_list, bias, res, g, b)


MLP_HIDDEN_CHUNK = 1024


def _mlp_kernel(x_ref, w1_ref, w2_ref, g_ref, b_ref, o_ref):
    x = x_ref[...]
    xb = x.astype(BF16)
    acc = DN_ALPHA * x
    for c0 in range(0, D_FF, MLP_HIDDEN_CHUNK):
        hid = jnp.dot(xb, w1_ref[:, c0:c0 + MLP_HIDDEN_CHUNK], preferred_element_type=F32)
        hid = jnp.maximum(hid, 0.0)
        hid = (hid * hid).astype(BF16)
        acc = acc + jnp.dot(hid, w2_ref[c0:c0 + MLP_HIDDEN_CHUNK, :], preferred_element_type=F32)
    o_ref[...] = _layer_norm(acc, g_ref[...], b_ref[...])


def _mlp(x, w1, w2, g, b, tm):
    rows = x.shape[0]
    return pl.pallas_call(
        _mlp_kernel,
        grid=(pl.cdiv(rows, tm),),
        in_specs=[
            pl.BlockSpec((tm, D_MODEL), lambda i: (i, 0)),
            pl.BlockSpec((D_MODEL, D_FF), lambda i: (0, 0)),
            pl.BlockSpec((D_FF, D_MODEL), lambda i: (0, 0)),
            pl.BlockSpec((1, D_MODEL), lambda i: (0, 0)),
            pl.BlockSpec((1, D_MODEL), lambda i: (0, 0)),
        ],
        out_specs=pl.BlockSpec((tm, D_MODEL), lambda i: (i, 0)),
        out_shape=jax.ShapeDtypeStruct((rows, D_MODEL), F32),
        compiler_params=_params(("parallel",)),
        name="mlp",
    )(x, w1, w2, g, b)


def _conv_mix_kernel(l_true, tl, rc,
                     main_ref, prev_ref, next_ref, sw_ref, sb_ref, dw_ref, db_ref, g_ref, b_ref,
                     x0_ref, z_ref, yb_ref, ext_ref, glu_ref):
    i = pl.program_id(1)
    row0 = i * tl - HALO

    def valid_rows(start, n):
        g = start + lax.broadcasted_iota(jnp.int32, (n, 1), 0)
        return (g >= 0) & (g < l_true)

    ext_ref[0:HALO, :] = jnp.where(valid_rows(row0, HALO), prev_ref[0], 0.0)
    ext_ref[HALO:HALO + tl, :] = jnp.where(valid_rows(row0 + HALO, tl), main_ref[0], 0.0)
    ext_ref[HALO + tl:, :] = jnp.where(valid_rows(row0 + HALO + tl, HALO), next_ref[0], 0.0)

    a = ext_ref[:, 3 * D_HYENA:3 * D_HYENA + D_CONF]
    gate = ext_ref[:, 3 * D_HYENA + D_CONF:]
    glu_ref[...] = a * jax.nn.sigmoid(gate)

    sw = sw_ref[...]
    sb = sb_ref[...]
    dw = dw_ref[...]
    half = (CONF_K - 1) // 2

    def chunk(c, carry):
        r0 = pl.multiple_of(c * rc, SUBLANES)
        rows_ok = valid_rows(i * tl + r0, rc)

        win = ext_ref[pl.ds(r0 + HALO - SUBLANES, rc + 2 * SUBLANES), 0:3 * D_HYENA]
        hy = sb
        for k in range(3):
            off = SUBLANES + k - 1
            hy = hy + sw[k:k + 1, :] * win[off:off + rc, :]
        x0_ref[0, pl.ds(r0, rc), :] = hy[:, 0:D_HYENA]
        zz = hy[:, D_HYENA:2 * D_HYENA] * hy[:, 2 * D_HYENA:]
        z_ref[0, pl.ds(r0, rc), :] = jnp.where(rows_ok, zz, 0.0)

        gwin = glu_ref[pl.ds(r0, rc + 2 * HALO), :]
        acc = db_ref[...]
        for j in range(SUBLANES):
            part = None
            for k in range(CONF_K):
                off = HALO + k - half
                if off % SUBLANES == j:
                    base = off - j
                    term = dw[k:k + 1, :] * gwin[base:base + rc + SUBLANES, :]
                    part = term if part is None else part + term
            acc = acc + part[j:j + rc, :]
        y = _layer_norm(acc, g_ref[...], b_ref[...])
        yb_ref[0, pl.ds(r0, rc), :] = y * jax.nn.sigmoid(y)
        return carry

    lax.fori_loop(0, tl // rc, chunk, 0)


def _conv_mix(proj, l_true, tl, short_w, short_b, dw_w, dw_b, ln_g, ln_b):
    bsz, lp, _ = proj.shape
    nt = lp // tl
    hb = tl // HALO
    rc = _divisor_tile(tl, 64, SUBLANES)
    out_sds = jax.ShapeDtypeStruct((bsz, lp, D_HYENA), F32)
    row_spec = lambda shape: pl.BlockSpec(shape, lambda b, i: (0, 0))
    return pl.pallas_call(
        functools.partial(_conv_mix_kernel, l_true, tl, rc),
        grid=(bsz, nt),
        in_specs=[
            pl.BlockSpec((1, tl, D_IN_EVEN), lambda b, i: (b, i, 0)),
            pl.BlockSpec((1, HALO, D_IN_EVEN), lambda b, i: (b, jnp.maximum(i * hb - 1, 0), 0)),
            pl.BlockSpec((1, HALO, D_IN_EVEN), lambda b, i: (b, jnp.minimum((i + 1) * hb, nt * hb - 1), 0)),
            row_spec((3, 3 * D_HYENA)),
            row_spec((1, 3 * D_HYENA)),
            row_spec((CONF_K, D_CONF)),
            row_spec((1, D_CONF)),
            row_spec((1, D_CONF)),
            row_spec((1, D_CONF)),
        ],
        out_specs=[pl.BlockSpec((1, tl, D_HYENA), lambda b, i: (b, i, 0))] * 3,
        out_shape=[out_sds, out_sds, out_sds],
        scratch_shapes=[
            pltpu.VMEM((tl + 2 * HALO, D_IN_EVEN), F32),
            pltpu.VMEM((tl + 2 * HALO, D_CONF), F32),
        ],
        compiler_params=_params(("parallel", "parallel")),
        name="conv_mix",
    )(proj, proj, proj, short_w, short_b, dw_w, dw_b, ln_g, ln_b)


def _filter_kernel(l_true, tl, zf_ref, w1_ref, b1_ref, f1_ref, w2_ref, b2_ref, f2_ref, w3_ref, dec_ref,
                   h_ref, ss_ref):
    i = pl.program_id(0)

    @pl.when(i == 0)
    def _():
        ss_ref[...] = jnp.zeros_like(ss_ref)

    zf = zf_ref[...]
    t = zf[:, 0:1]
    g = i * tl + lax.broadcasted_iota(jnp.int32, (tl, 1), 0)
    total = jnp.zeros((1, D_HYENA), F32)
    for d in range(2):
        a = jnp.dot(zf, w1_ref[d], precision=HIGHEST, preferred_element_type=F32) + b1_ref[d]
        a = jnp.sin(f1_ref[d] * a)
        a = jnp.dot(a, w2_ref[d], precision=HIGHEST, preferred_element_type=F32) + b2_ref[d]
        a = jnp.sin(f2_ref[d] * a)
        hh = jnp.dot(a, w3_ref[d], precision=HIGHEST, preferred_element_type=F32)
        hh = hh * jnp.exp(-t * dec_ref[d])
        keep = (g < l_true) & (g >= d)
        hh = jnp.where(keep, hh, 0.0)
        h_ref[d] = hh
        total = total + jnp.sum(hh * hh, axis=0, keepdims=True)
    ss_ref[...] += total


def _hyena_filter(zf, l_true, tl, w1, b1, f1, w2, b2, f2, w3, decay):
    lp = zf.shape[0]
    full = lambda shape: pl.BlockSpec(shape, lambda i: (0,) * len(shape))
    return pl.pallas_call(
        functools.partial(_filter_kernel, l_true, tl),
        grid=(lp // tl,),
        in_specs=[
            pl.BlockSpec((tl, FILTER_EMB_PAD), lambda i: (i, 0)),
            full((2, FILTER_EMB_PAD, FILTER_HIDDEN)),
            full((2, 1, FILTER_HIDDEN)),
            full((2, 1, FILTER_HIDDEN)),
            full((2, FILTER_HIDDEN, FILTER_HIDDEN)),
            full((2, 1, FILTER_HIDDEN)),
            full((2, 1, FILTER_HIDDEN)),
            full((2, FILTER_HIDDEN, D_HYENA)),
            full((2, 1, D_HYENA)),
        ],
        out_specs=[
            pl.BlockSpec((2, tl, D_HYENA), lambda i: (0, i, 0)),
            pl.BlockSpec((1, D_HYENA), lambda i: (0, 0)),
        ],
        out_shape=[
            jax.ShapeDtypeStruct((2, lp, D_HYENA), F32),
            jax.ShapeDtypeStruct((1, D_HYENA), F32),
        ],
        compiler_params=_params(("arbitrary",)),
        name="hyena_filter",
    )(zf, w1, b1, f1, w2, b2, f2, w3, decay)


S2_BLOCK = SUBLANES
BF16_ROWS = 2 * SUBLANES


def _split_bf16(x):
    hi = x.astype(BF16)
    return hi, (x - hi.astype(F32)).astype(BF16)


def _dot3(m_ref, x):
    x_hi, x_lo = _split_bf16(x)
    dot = functools.partial(jnp.dot, preferred_element_type=F32)
    return dot(m_ref[0], x_hi) + (dot(m_ref[0], x_lo) + dot(m_ref[1], x_hi))


def _dft_a_kernel(kz, kzp, f_ref, x_ref, o_ref):
    for j in range(S2_BLOCK):
        x = x_ref[0, :, j, :]
        if kzp != kz:
            x = jnp.concatenate([x, jnp.zeros((kzp - kz, x.shape[1]), F32)], axis=0)
        o_ref[0, :, j, :] = _dot3(f_ref, x)


def _dft_a(fmat, x):
    bsz, kz, n2, ch = x.shape
    _, m, kzp = fmat.shape
    return pl.pallas_call(
        functools.partial(_dft_a_kernel, kz, kzp),
        grid=(bsz, n2 // S2_BLOCK),
        in_specs=[
            pl.BlockSpec((2, m, kzp), lambda b, j: (0, 0, 0)),
            pl.BlockSpec((1, kz, S2_BLOCK, ch), lambda b, j: (b, 0, j, 0)),
        ],
        out_specs=pl.BlockSpec((1, m, S2_BLOCK, ch), lambda b, j: (b, 0, j, 0)),
        out_shape=jax.ShapeDtypeStruct((bsz, m, n2, ch), F32),
        compiler_params=_params(("parallel", "parallel")),
        name="dft_stage_a",
    )(fmat, x)


def _twiddle(ar, ai, twr, twi):
    return ar * twr - ai * twi, ar * twi + ai * twr


def _dft_b_filter_kernel(m_ref, twr_ref, twi_ref, a_ref, rs_ref, kf_ref):
    twr, twi = twr_ref[0], twi_ref[0]
    fwd = jnp.concatenate(_twiddle(a_ref[0, 0, 0], a_ref[0, 1, 0], twr, twi), axis=0)
    bwd = jnp.concatenate(_twiddle(a_ref[1, 0, 0], a_ref[1, 1, 0], twr, twi), axis=0)
    zf = _dot3(m_ref, fwd)
    zb = _dot3(m_ref, bwd)
    rs = rs_ref[...]
    kf_ref[0, 0] = (zf[:FFT_N2] + zb[:FFT_N2]) * rs
    kf_ref[1, 0] = (zf[FFT_N2:] - zb[FFT_N2:]) * rs


def _dft_b_filter(mc, twr, twi, a, rs):
    n1 = twr.shape[0]
    a5 = a.reshape(2, 2, n1, FFT_N2, D_HYENA)
    tw_spec = pl.BlockSpec((1, FFT_N2, 1), lambda f: (f, 0, 0))
    return pl.pallas_call(
        _dft_b_filter_kernel,
        grid=(n1,),
        in_specs=[
            pl.BlockSpec((2, 2 * FFT_N2, 2 * FFT_N2), lambda f: (0, 0, 0)),
            tw_spec, tw_spec,
            pl.BlockSpec((2, 2, 1, FFT_N2, D_HYENA), lambda f: (0, 0, f, 0, 0)),
            pl.BlockSpec((1, D_HYENA), lambda f: (0, 0)),
        ],
        out_specs=pl.BlockSpec((2, 1, FFT_N2, D_HYENA), lambda f: (0, f, 0, 0)),
        out_shape=jax.ShapeDtypeStruct((2, n1, FFT_N2, D_HYENA), F32),
        compiler_params=_params(("parallel",)),
        name="dft_filter_spectrum",
    )(mc, twr, twi, a5, rs)


def _dft_b_kernel(m_ref, mt_ref, twr_ref, twi_ref, a_ref, kf_ref, o_ref):
    twr, twi = twr_ref[0], twi_ref[0]
    x = jnp.concatenate(_twiddle(a_ref[0, 0, 0], a_ref[0, 1, 0], twr, twi), axis=0)
    z = _dot3(m_ref, x)
    y = jnp.concatenate(_twiddle(z[:FFT_N2], z[FFT_N2:], kf_ref[0, 0], kf_ref[1, 0]), axis=0)
    w = _dot3(mt_ref, y)
    o_ref[0, 0, 0], o_ref[0, 1, 0] = _twiddle(w[:FFT_N2], w[FFT_N2:], twr, -twi)


def _dft_b(mc, mct, twr, twi, a, kf):
    bsz = a.shape[0]
    n1 = twr.shape[0]
    a5 = a.reshape(bsz, 2, n1, FFT_N2, D_HYENA)
    mat_spec = pl.BlockSpec((2, 2 * FFT_N2, 2 * FFT_N2), lambda b, f: (0, 0, 0))
    tw_spec = pl.BlockSpec((1, FFT_N2, 1), lambda b, f: (f, 0, 0))
    out = pl.pallas_call(
        _dft_b_kernel,
        grid=(bsz, n1),
        in_specs=[
            mat_spec, mat_spec, tw_spec, tw_spec,
            pl.BlockSpec((1, 2, 1, FFT_N2, D_HYENA), lambda b, f: (b, 0, f, 0, 0)),
            pl.BlockSpec((2, 1, FFT_N2, D_HYENA), lambda b, f: (0, f, 0, 0)),
        ],
        out_specs=pl.BlockSpec((1, 2, 1, FFT_N2, D_HYENA), lambda b, f: (b, 0, f, 0, 0)),
        out_shape=jax.ShapeDtypeStruct((bsz, 2, n1, FFT_N2, D_HYENA), F32),
        compiler_params=_params(("parallel", "parallel")),
        name="dft_stage_b",
    )(mc, mct, twr, twi, a5, kf)
    return out.reshape(bsz, 2 * n1, FFT_N2, D_HYENA)


def _dft_c_kernel(kz, g_ref, w_ref, x0_ref, z_ref, d_ref, o_ref):
    for j in range(S2_BLOCK):
        y = jnp.dot(g_ref[...], w_ref[0, :, j, :], precision=HIGHEST, preferred_element_type=F32)
        o_ref[0, :, j, :] = x0_ref[0, :, j, :] * (y[:kz] + z_ref[0, :, j, :] * d_ref[...])


def _dft_c(gmat, w, x0, z, skip_d):
    bsz, kz, n2, ch = x0.shape
    kzp, m = gmat.shape
    seq_spec = pl.BlockSpec((1, kz, S2_BLOCK, ch), lambda b, j: (b, 0, j, 0))
    return pl.pallas_call(
        functools.partial(_dft_c_kernel, kz),
        grid=(bsz, n2 // S2_BLOCK),
        in_specs=[
            pl.BlockSpec((kzp, m), lambda b, j: (0, 0)),
            pl.BlockSpec((1, m, S2_BLOCK, ch), lambda b, j: (b, 0, j, 0)),
            seq_spec, seq_spec,
            pl.BlockSpec((1, ch), lambda b, j: (0, 0)),
        ],
        out_specs=seq_spec,
        out_shape=jax.ShapeDtypeStruct((bsz, kz, n2, ch), F32),
        compiler_params=_params(("parallel", "parallel")),
        name="dft_stage_c",
    )(gmat, w, x0, z, skip_d)


def _dft_tables(n1, kz, kzp):
    assert n1 % 2 == 0
    n2 = FFT_N2
    n = n1 * n2
    f1 = np.arange(_round_up(n1 // 2 + 1, SUBLANES))
    s1 = np.arange(kzp)
    ang = (2.0 * np.pi / n1) * ((f1[:, None] * s1[None, :]) % n1)
    live = (s1 < kz)[None, :]
    fa = np.concatenate([np.where(live, np.cos(ang), 0.0), np.where(live, -np.sin(ang), 0.0)], axis=0)
    weight = np.where((f1 == 0) | (f1 == n1 // 2), 1.0, np.where(f1 < n1 // 2, 2.0, 0.0))
    ga = np.concatenate([np.cos(ang).T * weight, -np.sin(ang).T * weight], axis=1)
    ga = ga * np.where(s1 < kz, 1.0 / n, 0.0)[:, None]
    idx = np.arange(n2)
    ang2 = (2.0 * np.pi / n2) * ((idx[:, None] * idx[None, :]) % n2)
    cr, ci = np.cos(ang2), -np.sin(ang2)
    mc = np.block([[cr, -ci], [ci, cr]])
    mct = np.block([[cr, ci], [-ci, cr]])
    ang3 = (2.0 * np.pi / n) * ((f1[:, None] * idx[None, :]) % n)
    twr, twi = np.cos(ang3)[:, :, None], -np.sin(ang3)[:, :, None]

    def hi_lo(mat):
        hi = mat.astype(BF16)
        lo = (mat - hi.astype(np.float64)).astype(BF16)
        return jnp.asarray(np.stack([hi, lo]))

    return hi_lo(fa), jnp.asarray(ga, F32), hi_lo(mc), hi_lo(mct), jnp.asarray(twr, F32), jnp.asarray(twi, F32)


def _rms(x, g):
    return x * lax.rsqrt(jnp.mean(x * x, axis=-1, keepdims=True) + RMS_EPS) * g


def _mla_proj_kernel(l_true, tm,
                     x_ref, cos_ref, sin_ref, wqa_ref, qn_ref, wq1_ref, wq2_ref, wc_ref, kn_ref,
                     wk1_ref, wk2_ref, wkk_ref, wkv_ref, q_ref, k_ref, v_ref):
    i = pl.program_id(1)
    x = x_ref[0].astype(BF16)
    cos = cos_ref[...]
    sin = sin_ref[...]
    lane = lax.broadcasted_iota(jnp.int32, (1, HEAD_SLOT), 1)

    cq = _rms(jnp.dot(x, wqa_ref[...], preferred_element_type=F32), qn_ref[...]).astype(BF16)
    q1 = jnp.dot(cq, wq1_ref[...], preferred_element_type=F32)
    q2 = jnp.dot(cq, wq2_ref[...], preferred_element_type=F32)
    q_one = jnp.where(lane == PAD_LANE, 1.0, 0.0)

    ckv = _rms(jnp.dot(x, wc_ref[...], preferred_element_type=F32), kn_ref[...]).astype(BF16)
    kn = jnp.dot(ckv, wkk_ref[...], preferred_element_type=F32)
    vv = jnp.dot(ckv, wkv_ref[...], preferred_element_type=F32)
    k1 = jnp.dot(x, wk1_ref[...], preferred_element_type=F32)
    k2 = jnp.dot(x, wk2_ref[...], preferred_element_type=F32)
    rows = i * tm + lax.broadcasted_iota(jnp.int32, (tm, 1), 0)
    k_pe = k1 * cos + k2 * sin + jnp.where((rows >= l_true) & (lane == PAD_LANE), PAD_SCORE, 0.0)
    v_one = jnp.where(lane == SUM_LANE, 1.0, 0.0)

    for h in range(N_HEADS):
        sl = slice(h * HEAD_SLOT, (h + 1) * HEAD_SLOT)
        q_ref[0, :, sl] = (q1[:, sl] * cos + q2[:, sl] * sin + q_one).astype(BF16)
        k_ref[0, :, sl] = (kn[:, sl] + k_pe).astype(BF16)
        v_ref[0, :, sl] = (vv[:, sl] + v_one).astype(BF16)


def _mla_proj(h, l_true, tm, cos_t, sin_t, wqa, qn, wq1, wq2, wc, kn, wk1, wk2, wkk, wkv):
    bsz, lp, _ = h.shape
    wide = N_HEADS * HEAD_SLOT
    full = lambda a: pl.BlockSpec(a.shape, lambda b, i: (0,) * a.ndim)
    out_sds = jax.ShapeDtypeStruct((bsz, lp, wide), BF16)
    weights = (wqa, qn, wq1, wq2, wc, kn, wk1, wk2, wkk, wkv)
    return pl.pallas_call(
        functools.partial(_mla_proj_kernel, l_true, tm),
        grid=(bsz, lp // tm),
        in_specs=[
            pl.BlockSpec((1, tm, D_MODEL), lambda b, i: (b, i, 0)),
            pl.BlockSpec((tm, HEAD_SLOT), lambda b, i: (i, 0)),
            pl.BlockSpec((tm, HEAD_SLOT), lambda b, i: (i, 0)),
        ] + [full(w) for w in weights],
        out_specs=[pl.BlockSpec((1, tm, wide), lambda b, i: (b, i, 0))] * 3,
        out_shape=[out_sds, out_sds, out_sds],
        compiler_params=_params(("parallel", "parallel")),
        name="mla_proj",
    )(h, cos_t, sin_t, *weights)


def _attn_kernel(tk, tr, q_ref, k_ref, v_ref, o_ref, s0_ref, s1_ref, m_ref, acc_ref):
    tq = q_ref.shape[1]
    nk = k_ref.shape[1] // tk
    s_refs = (s0_ref, s1_ref)
    row_blocks = [slice(r * tr, (r + 1) * tr) for r in range(tq // tr)]

    def scores(c, slot):
        k = k_ref[0, pl.ds(pl.multiple_of(c * tk, tk), tk), :]
        for rows in row_blocks:
            s_refs[slot][rows, :] = lax.dot_general(q_ref[0, rows, :], k, (((1,), (1,)), ((), ())),
                                                    preferred_element_type=F32)

    def softmax_pv(c, slot):
        v = v_ref[0, pl.ds(pl.multiple_of(c * tk, tk), tk), :]
        for rows in row_blocks:
            s = s_refs[slot][rows, :]
            m_old = m_ref[rows, :]
            m_new = jnp.maximum(m_old, jnp.max(s, axis=-1, keepdims=True))
            p = jnp.exp2(s - m_new).astype(BF16)
            acc_ref[rows, :] = (jnp.exp2(m_old - m_new) * acc_ref[rows, :]
                                + jnp.dot(p, v, preferred_element_type=F32))
            m_ref[rows, :] = m_new

    def step(c, slot):
        scores(c + 1, 1 - slot)
        softmax_pv(c, slot)

    m_ref[...] = jnp.full_like(m_ref, -jnp.inf)
    acc_ref[...] = jnp.zeros_like(acc_ref)
    scores(0, 0)

    def pair(j, carry):
        step(2 * j, 0)
        step(2 * j + 1, 1)
        return carry

    lax.fori_loop(0, (nk - 1) // 2, pair, 0)
    if (nk - 1) % 2:
        step(nk - 2, (nk - 2) % 2)
    softmax_pv(nk - 1, (nk - 1) % 2)
    acc = acc_ref[...]
    o_ref[0] = (acc / acc[:, SUM_LANE:SUM_LANE + 1]).astype(o_ref.dtype)


def _attention(q, k, v, tq, tk, tr):
    bsz, lp, wide = q.shape
    return pl.pallas_call(
        functools.partial(_attn_kernel, tk, tr),
        grid=(bsz, N_HEADS, lp // tq),
        in_specs=[
            pl.BlockSpec((1, tq, HEAD_SLOT), lambda b, h, i: (b, i, h)),
            pl.BlockSpec((1, lp, HEAD_SLOT), lambda b, h, i: (b, 0, h)),
            pl.BlockSpec((1, lp, HEAD_SLOT), lambda b, h, i: (b, 0, h)),
        ],
        out_specs=pl.BlockSpec((1, tq, HEAD_SLOT), lambda b, h, i: (b, i, h)),
        out_shape=jax.ShapeDtypeStruct((bsz, lp, wide), BF16),
        scratch_shapes=[pltpu.VMEM((tq, tk), F32), pltpu.VMEM((tq, tk), F32),
                        pltpu.VMEM((tq, 1), F32), pltpu.VMEM((tq, HEAD_SLOT), F32)],
        compiler_params=_params(("parallel", "parallel", "arbitrary")),
        name="attention",
    )(q, k, v)


def _slot_cols(w, per_head, pieces):
    k = w.shape[0]
    wh = w.reshape(k, N_HEADS, per_head)
    out = jnp.zeros((k, N_HEADS, HEAD_SLOT), w.dtype)
    for dst, src, width, sign in pieces:
        out = out.at[:, :, dst:dst + width].set(sign * wh[:, :, src:src + width])
    return out.reshape(k, N_HEADS * HEAD_SLOT)


def _mla_weights(wq_b, wkv_a, wkv_b, wo):
    half = QK_ROPE // 2
    qk = QK_NOPE + QK_ROPE
    q_scale = (qk ** -0.5) * math.log2(math.e)
    wq1 = _slot_cols(wq_b, qk, [(0, 0, QK_NOPE, 1.0), (QK_NOPE, QK_NOPE, QK_ROPE, 1.0)]) * q_scale
    wq2 = _slot_cols(wq_b, qk, [(QK_NOPE, QK_NOPE + half, half, -1.0),
                                (QK_NOPE + half, QK_NOPE, half, 1.0)]) * q_scale
    wc = wkv_a[:, :KV_RANK]
    pe = wkv_a[:, KV_RANK:]
    zeros = lambda n: jnp.zeros((D_MODEL, n), wkv_a.dtype)
    tail = HEAD_SLOT - QK_NOPE - QK_ROPE
    wk1 = jnp.concatenate([zeros(QK_NOPE), pe, zeros(tail)], axis=1)
    wk2 = jnp.concatenate([zeros(QK_NOPE), -pe[:, half:], pe[:, :half], zeros(tail)], axis=1)
    wkk = _slot_cols(wkv_b, QK_NOPE + V_HEAD, [(0, 0, QK_NOPE, 1.0)])
    wkv = _slot_cols(wkv_b, QK_NOPE + V_HEAD, [(0, QK_NOPE, V_HEAD, 1.0)])
    wo_slot = jnp.zeros((N_HEADS, HEAD_SLOT, D_MODEL), wo.dtype)
    wo_slot = wo_slot.at[:, :V_HEAD, :].set(wo.reshape(N_HEADS, V_HEAD, D_MODEL))
    wo_slot = wo_slot.reshape(N_HEADS * HEAD_SLOT, D_MODEL)
    cast = lambda a: a.astype(BF16)
    return tuple(map(cast, (wq1, wq2, wc, wk1, wk2, wkk, wkv, wo_slot)))


def _rope_slot_tables(lp):
    pos = jnp.arange(lp, dtype=F32)
    inv = 1.0 / (ROPE_THETA ** (jnp.arange(0, QK_ROPE, 2, dtype=F32) / QK_ROPE))
    ang = pos[:, None] * inv[None, :]
    cos, sin = jnp.cos(ang), jnp.sin(ang)
    tail = HEAD_SLOT - QK_NOPE - QK_ROPE
    cos_t = jnp.concatenate([jnp.ones((lp, QK_NOPE), F32), cos, cos, jnp.zeros((lp, tail), F32)], axis=1)
    sin_t = jnp.concatenate([jnp.zeros((lp, QK_NOPE), F32), sin, sin, jnp.zeros((lp, tail), F32)], axis=1)
    return cos_t, sin_t


def _filter_features(l_true, lp):
    t = jnp.arange(lp, dtype=F32) / max(l_true - 1, 1)
    bands = (FILTER_EMB - 1) // 2
    freqs = jnp.linspace(1e-4, bands - 1, bands, dtype=F32)
    w = 2.0 * math.pi * jnp.arange(lp, dtype=F32) / l_true
    ang = w[:, None] * freqs[None, :]
    z = jnp.concatenate([t[:, None], jnp.cos(ang), -jnp.sin(ang)], axis=-1)
    return jnp.pad(z, ((0, 0), (0, FILTER_EMB_PAD - FILTER_EMB)))


def _divisor_tile(total, target, multiple):
    best = None
    for d in range(multiple, min(total, target) + 1, multiple):
        if total % d == 0:
            best = d
    assert best is not None, (total, target, multiple)
    return best


SHORT_SEQ_MAX = 2304
MXU_COLS = 256


def _tiling(bsz, l_true):
    lp = _round_up(l_true, LANES if l_true <= SHORT_SEQ_MAX else MXU_COLS)
    tk = _divisor_tile(lp, 1280, LANES // 2)
    tq = _divisor_tile(lp, 1280, 2 * SUBLANES)
    return dict(
        lp=lp, tk=tk,
        tl=_divisor_tile(lp, 640, HALO),
        tq=tq, tr=_divisor_tile(tq, 256, 2 * SUBLANES),
        tm=_divisor_tile(bsz * lp, 1088, 2 * SUBLANES),
        tmlp=_divisor_tile(bsz * lp, 640, 2 * SUBLANES),
    )


def _trunk(x, p):
    bsz, seq, _ = x.shape
    l_true = seq + N_META
    cfg = _tiling(bsz, l_true)
    lp = cfg["lp"]
    rows = bsz * lp

    meta = jnp.broadcast_to(p["meta_tokens"][None], (bsz, N_META, D_MODEL))
    h = jnp.concatenate([meta, x, jnp.zeros((bsz, lp - l_true, D_MODEL), F32)], axis=1)
    h = h.reshape(rows, D_MODEL)

    proj = _mm_bias(h, p["ev_w_in"], p["ev_b_in"], cfg["tm"]).reshape(bsz, lp, D_IN_EVEN)
    x0, z, y_b = _conv_mix(proj, l_true, cfg["tl"], p["ev_short_w"], p["ev_short_b"],
                           p["cf_dw_w"], p["cf_dw_b"], p["cf_ln_g"], p["cf_ln_b"])

    kz = lp // FFT_N2
    kzp = _round_up(kz, BF16_ROWS)
    n1 = _round_up(-(-(2 * l_true - 1) // FFT_N2), SUBLANES)
    fa, ga, mc, mct, twr, twi = _dft_tables(n1, kz, kzp)
    seq4 = lambda a: a.reshape(a.shape[0], kz, FFT_N2, D_HYENA)

    zf = _filter_features(l_true, lp)
    taps, sumsq = _hyena_filter(zf, l_true, cfg["tl"], p["hy_w1"], p["hy_b1"], p["hy_freq1"], p["hy_w2"],
                                p["hy_b2"], p["hy_freq2"], p["hy_w3"], p["hy_decay"])
    rs = lax.rsqrt(sumsq)
    kf = _dft_b_filter(mc, twr, twi, _dft_a(fa, seq4(taps)), rs)

    wb = _dft_b(mc, mct, twr, twi, _dft_a(fa, seq4(z)), kf)
    y_a = _dft_c(ga, wb, seq4(x0), seq4(z), p["hy_skip_d"]).reshape(rows, D_HYENA)

    h = _mm_res_ln([y_a, y_b.reshape(rows, D_CONF)], [p["ev_w_out_a"], p["ev_w_out_b"]], p["ev_b_out"],
                   h, p["ln1_g"][0], p["ln1_b"][0], cfg["tm"], "out_proj_even")
    h = _mlp(h, p["mlp_w1"][0], p["mlp_w2"][0], p["ln2_g"][0], p["ln2_b"][0], cfg["tmlp"])

    cos_t, sin_t = _rope_slot_tables(lp)
    q, k, v = _mla_proj(h.reshape(bsz, lp, D_MODEL), l_true, cfg["tl"], cos_t, sin_t,
                        p["mla_wq_a"], p["mla_q_norm"], p["wq1"], p["wq2"], p["wc"], p["mla_kv_norm"],
                        p["wk1"], p["wk2"], p["wkk"], p["wkv"])
    o = _attention(q, k, v, cfg["tq"], cfg["tk"], cfg["tr"]).reshape(rows, N_HEADS * HEAD_SLOT)
    h = _mm_res_ln([o], [p["wo_slot"]], jnp.zeros((1, D_MODEL), F32), h, p["ln1_g"][1], p["ln1_b"][1],
                   cfg["tm"], "out_proj_odd")
    h = _mlp(h, p["mlp_w1"][1], p["mlp_w2"][1], p["ln2_g"][1], p["ln2_b"][1], cfg["tmlp"])

    return h.reshape(bsz, lp, D_MODEL)[:, N_META:l_true]


def kernel(x_prompt, x_sample, meta_tokens, ev_w_in, ev_b_in, ev_short_w, ev_short_b, hy_w1, hy_b1, hy_freq1,
           hy_w2, hy_b2, hy_freq2, hy_w3, hy_decay, hy_skip_d, cf_dw_w, cf_dw_b, cf_ln_g, cf_ln_b, ev_w_out,
           ev_b_out, mla_wq_a, mla_q_norm, mla_wq_b, mla_wkv_a, mla_kv_norm, mla_wkv_b, mla_wo, ln1_g, ln1_b,
           mlp_w1, mlp_w2, ln2_g, ln2_b):
    row = lambda a: a.reshape(1, -1)
    wq1, wq2, wc, wk1, wk2, wkk, wkv, wo_slot = _mla_weights(mla_wq_b[0], mla_wkv_a[0], mla_wkv_b[0], mla_wo[0])
    p = dict(
        meta_tokens=meta_tokens,
        ev_w_in=ev_w_in[0].astype(BF16), ev_b_in=row(ev_b_in[0]),
        ev_short_w=ev_short_w[0], ev_short_b=row(ev_short_b[0]),
        hy_w1=jnp.pad(hy_w1[0], ((0, 0), (0, FILTER_EMB_PAD - FILTER_EMB), (0, 0))),
        hy_b1=hy_b1[0][:, None, :], hy_freq1=hy_freq1[0][:, None, :],
        hy_w2=hy_w2[0], hy_b2=hy_b2[0][:, None, :], hy_freq2=hy_freq2[0][:, None, :],
        hy_w3=hy_w3[0], hy_decay=hy_decay[0][:, None, :], hy_skip_d=row(hy_skip_d[0]),
        cf_dw_w=cf_dw_w[0], cf_dw_b=row(cf_dw_b[0]), cf_ln_g=row(cf_ln_g[0]), cf_ln_b=row(cf_ln_b[0]),
        ev_w_out_a=ev_w_out[0, :D_HYENA].astype(BF16), ev_w_out_b=ev_w_out[0, D_HYENA:].astype(BF16),
        ev_b_out=row(ev_b_out[0]),
        mla_wq_a=mla_wq_a[0].astype(BF16), mla_q_norm=row(mla_q_norm[0]), mla_kv_norm=row(mla_kv_norm[0]),
        wq1=wq1, wq2=wq2, wc=wc, wk1=wk1, wk2=wk2, wkk=wkk, wkv=wkv, wo_slot=wo_slot,
        ln1_g=ln1_g[:, None, :], ln1_b=ln1_b[:, None, :], ln2_g=ln2_g[:, None, :], ln2_b=ln2_b[:, None, :],
        mlp_w1=mlp_w1.astype(BF16), mlp_w2=mlp_w2.astype(BF16),
    )
    return (_trunk(x_prompt, p), _trunk(x_sample, p))
```

```python
import functools
import math

import jax
import jax.numpy as jnp
import numpy as np
from jax import lax
from jax.experimental import pallas as pl
from jax.experimental.pallas import tpu as pltpu

F32 = jnp.float32
BF16 = jnp.bfloat16

D_MODEL = 1024
N_META = 16
D_HYENA = 512
D_CONF = 512
D_IN_EVEN = 3 * D_HYENA + 2 * D_CONF
CONF_K = 31
FILTER_EMB = 33
FILTER_EMB_PAD = 40
FILTER_HIDDEN = 64
N_HEADS = 16
QK_NOPE = 64
QK_ROPE = 32
V_HEAD = 64
Q_RANK = 384
KV_RANK = 256
ROPE_THETA = 10000.0
D_FF = 4096
DEPTH = 2
DN_ALPHA = (2 * DEPTH) ** 0.25
LN_EPS = 1e-5
RMS_EPS = 1e-6

LANES = 128
SUBLANES = 8
HEAD_SLOT = LANES
HALO = 16
VMEM_LIMIT = 56 * 1024 * 1024

PAD_LANE = QK_NOPE + QK_ROPE
SUM_LANE = V_HEAD
PAD_SCORE = -1e30
FFT_N2 = 128

HIGHEST = lax.Precision.HIGHEST


def _params(sem, vmem=VMEM_LIMIT):
    return pltpu.CompilerParams(dimension_semantics=sem, vmem_limit_bytes=vmem)


def _round_up(x, m):
    return -(-x // m) * m


def _layer_norm(x, g, b):
    mu = jnp.mean(x, axis=-1, keepdims=True)
    xc = x - mu
    var = jnp.mean(xc * xc, axis=-1, keepdims=True)
    return xc * lax.rsqrt(var + LN_EPS) * g + b


def _mm_bias_kernel(x_ref, w_ref, b_ref, o_ref):
    acc = jnp.dot(x_ref[...].astype(BF16), w_ref[...], preferred_element_type=F32)
    o_ref[...] = (acc + b_ref[...]).astype(o_ref.dtype)


def _mm_bias(x, w, b, tm, out_dtype=F32):
    rows, k = x.shape
    n = w.shape[1]
    return pl.pallas_call(
        _mm_bias_kernel,
        grid=(pl.cdiv(rows, tm),),
        in_specs=[
            pl.BlockSpec((tm, k), lambda i: (i, 0)),
            pl.BlockSpec((k, n), lambda i: (0, 0)),
            pl.BlockSpec((1, n), lambda i: (0, 0)),
        ],
        out_specs=pl.BlockSpec((tm, n), lambda i: (i, 0)),
        out_shape=jax.ShapeDtypeStruct((rows, n), out_dtype),
        compiler_params=_params(("parallel",)),
        name="in_proj",
    )(x, w, b)


def _mm_res_ln_kernel(n_in, *refs):
    a_refs = refs[:n_in]
    w_refs = refs[n_in:2 * n_in]
    bias_ref, res_ref, g_ref, b_ref, o_ref = refs[2 * n_in:]
    acc = bias_ref[...] + DN_ALPHA * res_ref[...]
    for a_ref, w_ref in zip(a_refs, w_refs):
        acc = acc + jnp.dot(a_ref[...].astype(BF16), w_ref[...], preferred_element_type=F32)
    o_ref[...] = _layer_norm(acc, g_ref[...], b_ref[...])


def _mm_res_ln(a_list, w_list, bias, res, g, b, tm, name):
    rows = res.shape[0]
    n_in = len(a_list)
    in_specs = [pl.BlockSpec((tm, a.shape[1]), lambda i: (i, 0)) for a in a_list]
    in_specs += [pl.BlockSpec(w.shape, lambda i: (0, 0)) for w in w_list]
    in_specs += [
        pl.BlockSpec((1, D_MODEL), lambda i: (0, 0)),
        pl.BlockSpec((tm, D_MODEL), lambda i: (i, 0)),
        pl.BlockSpec((1, D_MODEL), lambda i: (0, 0)),
        pl.BlockSpec((1, D_MODEL), lambda i: (0, 0)),
    ]
    return pl.pallas_call(
        functools.partial(_mm_res_ln_kernel, n_in),
        grid=(pl.cdiv(rows, tm),),
        in_specs=in_specs,
        out_specs=pl.BlockSpec((tm, D_MODEL), lambda i: (i, 0)),
        out_shape=jax.ShapeDtypeStruct((rows, D_MODEL), F32),
        compiler_params=_params(("parallel",)),
        name=name,
    )(*a_list, *w_list, bias, res, g, b)


MLP_HIDDEN_CHUNK = 1024


def _mlp_kernel(x_ref, w1_ref, w2_ref, g_ref, b_ref, o_ref):
    x = x_ref[...]
    xb = x.astype(BF16)
    acc = DN_ALPHA * x
    for c0 in range(0, D_FF, MLP_HIDDEN_CHUNK):
        hid = jnp.dot(xb, w1_ref[:, c0:c0 + MLP_HIDDEN_CHUNK], preferred_element_type=F32)
        hid = jnp.maximum(hid, 0.0)
        hid = (hid * hid).astype(BF16)
        acc = acc + jnp.dot(hid, w2_ref[c0:c0 + MLP_HIDDEN_CHUNK, :], preferred_element_type=F32)
    o_ref[...] = _layer_norm(acc, g_ref[...], b_ref[...])


def _mlp(x, w1, w2, g, b, tm):
    rows = x.shape[0]
    return pl.pallas_call(
        _mlp_kernel,
        grid=(pl.cdiv(rows, tm),),
        in_specs=[
            pl.BlockSpec((tm, D_MODEL), lambda i: (i, 0)),
            pl.BlockSpec((D_MODEL, D_FF), lambda i: (0, 0)),
            pl.BlockSpec((D_FF, D_MODEL), lambda i: (0, 0)),
            pl.BlockSpec((1, D_MODEL), lambda i: (0, 0)),
            pl.BlockSpec((1, D_MODEL), lambda i: (0, 0)),
        ],
        out_specs=pl.BlockSpec((tm, D_MODEL), lambda i: (i, 0)),
        out_shape=jax.ShapeDtypeStruct((rows, D_MODEL), F32),
        compiler_params=_params(("parallel",)),
        name="mlp",
    )(x, w1, w2, g, b)


def _conv_mix_kernel(l_true, tl, rc,
                     main_ref, prev_ref, next_ref, sw_ref, sb_ref, dw_ref, db_ref, g_ref, b_ref,
                     x0_ref, z_ref, yb_ref, ext_ref, glu_ref):
    i = pl.program_id(1)
    row0 = i * tl - HALO

    def valid_rows(start, n):
        g = start + lax.broadcasted_iota(jnp.int32, (n, 1), 0)
        return (g >= 0) & (g < l_true)

    ext_ref[0:HALO, :] = jnp.where(valid_rows(row0, HALO), prev_ref[0], 0.0)
    ext_ref[HALO:HALO + tl, :] = jnp.where(valid_rows(row0 + HALO, tl), main_ref[0], 0.0)
    ext_ref[HALO + tl:, :] = jnp.where(valid_rows(row0 + HALO + tl, HALO), next_ref[0], 0.0)

    a = ext_ref[:, 3 * D_HYENA:3 * D_HYENA + D_CONF]
    gate = ext_ref[:, 3 * D_HYENA + D_CONF:]
    glu_ref[...] = a * jax.nn.sigmoid(gate)

    sw = sw_ref[...]
    sb = sb_ref[...]
    dw = dw_ref[...]
    half = (CONF_K - 1) // 2

    def chunk(c, carry):
        r0 = pl.multiple_of(c * rc, SUBLANES)
        rows_ok = valid_rows(i * tl + r0, rc)

        win = ext_ref[pl.ds(r0 + HALO - SUBLANES, rc + 2 * SUBLANES), 0:3 * D_HYENA]
        hy = sb
        for k in range(3):
            off = SUBLANES + k - 1
            hy = hy + sw[k:k + 1, :] * win[off:off + rc, :]
        x0_ref[0, pl.ds(r0, rc), :] = hy[:, 0:D_HYENA]
        zz = hy[:, D_HYENA:2 * D_HYENA] * hy[:, 2 * D_HYENA:]
        z_ref[0, pl.ds(r0, rc), :] = jnp.where(rows_ok, zz, 0.0)

        gwin = glu_ref[pl.ds(r0, rc + 2 * HALO), :]
        acc = db_ref[...]
        for j in range(SUBLANES):
            part = None
            for k in range(CONF_K):
                off = HALO + k - half
                if off % SUBLANES == j:
                    base = off - j
                    term = dw[k:k + 1, :] * gwin[base:base + rc + SUBLANES, :]
                    part = term if part is None else part + term
            acc = acc + part[j:j + rc, :]
        y = _layer_norm(acc, g_ref[...], b_ref[...])
        yb_ref[0, pl.ds(r0, rc), :] = y * jax.nn.sigmoid(y)
        return carry

    lax.fori_loop(0, tl // rc, chunk, 0)


def _conv_mix(proj, l_true, tl, short_w, short_b, dw_w, dw_b, ln_g, ln_b):
    bsz, lp, _ = proj.shape
    nt = lp // tl
    hb = tl // HALO
    rc = _divisor_tile(tl, 64, SUBLANES)
    out_sds = jax.ShapeDtypeStruct((bsz, lp, D_HYENA), F32)
    row_spec = lambda shape: pl.BlockSpec(shape, lambda b, i: (0, 0))
    return pl.pallas_call(
        functools.partial(_conv_mix_kernel, l_true, tl, rc),
        grid=(bsz, nt),
        in_specs=[
            pl.BlockSpec((1, tl, D_IN_EVEN), lambda b, i: (b, i, 0)),
            pl.BlockSpec((1, HALO, D_IN_EVEN), lambda b, i: (b, jnp.maximum(i * hb - 1, 0), 0)),
            pl.BlockSpec((1, HALO, D_IN_EVEN), lambda b, i: (b, jnp.minimum((i + 1) * hb, nt * hb - 1), 0)),
            row_spec((3, 3 * D_HYENA)),
            row_spec((1, 3 * D_HYENA)),
            row_spec((CONF_K, D_CONF)),
            row_spec((1, D_CONF)),
            row_spec((1, D_CONF)),
            row_spec((1, D_CONF)),
        ],
        out_specs=[pl.BlockSpec((1, tl, D_HYENA), lambda b, i: (b, i, 0))] * 3,
        out_shape=[out_sds, out_sds, out_sds],
        scratch_shapes=[
            pltpu.VMEM((tl + 2 * HALO, D_IN_EVEN), F32),
            pltpu.VMEM((tl + 2 * HALO, D_CONF), F32),
        ],
        compiler_params=_params(("parallel", "parallel")),
        name="conv_mix",
    )(proj, proj, proj, short_w, short_b, dw_w, dw_b, ln_g, ln_b)


def _filter_kernel(l_true, tl, zf_ref, w1_ref, b1_ref, f1_ref, w2_ref, b2_ref, f2_ref, w3_ref, dec_ref,
                   h_ref, ss_ref):
    i = pl.program_id(0)

    @pl.when(i == 0)
    def _():
        ss_ref[...] = jnp.zeros_like(ss_ref)

    zf = zf_ref[...]
    t = zf[:, 0:1]
    g = i * tl + lax.broadcasted_iota(jnp.int32, (tl, 1), 0)
    total = jnp.zeros((1, D_HYENA), F32)
    for d in range(2):
        a = jnp.dot(zf, w1_ref[d], precision=HIGHEST, preferred_element_type=F32) + b1_ref[d]
        a = jnp.sin(f1_ref[d] * a)
        a = jnp.dot(a, w2_ref[d], precision=HIGHEST, preferred_element_type=F32) + b2_ref[d]
        a = jnp.sin(f2_ref[d] * a)
        hh = jnp.dot(a, w3_ref[d], precision=HIGHEST, preferred_element_type=F32)
        hh = hh * jnp.exp(-t * dec_ref[d])
        keep = (g < l_true) & (g >= d)
        hh = jnp.where(keep, hh, 0.0)
        h_ref[d] = hh
        total = total + jnp.sum(hh * hh, axis=0, keepdims=True)
    ss_ref[...] += total


def _hyena_filter(zf, l_true, tl, w1, b1, f1, w2, b2, f2, w3, decay):
    lp = zf.shape[0]
    full = lambda shape: pl.BlockSpec(shape, lambda i: (0,) * len(shape))
    return pl.pallas_call(
        functools.partial(_filter_kernel, l_true, tl),
        grid=(lp // tl,),
        in_specs=[
            pl.BlockSpec((tl, FILTER_EMB_PAD), lambda i: (i, 0)),
            full((2, FILTER_EMB_PAD, FILTER_HIDDEN)),
            full((2, 1, FILTER_HIDDEN)),
            full((2, 1, FILTER_HIDDEN)),
            full((2, FILTER_HIDDEN, FILTER_HIDDEN)),
            full((2, 1, FILTER_HIDDEN)),
            full((2, 1, FILTER_HIDDEN)),
            full((2, FILTER_HIDDEN, D_HYENA)),
            full((2, 1, D_HYENA)),
        ],
        out_specs=[
            pl.BlockSpec((2, tl, D_HYENA), lambda i: (0, i, 0)),
            pl.BlockSpec((1, D_HYENA), lambda i: (0, 0)),
        ],
        out_shape=[
            jax.ShapeDtypeStruct((2, lp, D_HYENA), F32),
            jax.ShapeDtypeStruct((1, D_HYENA), F32),
        ],
        compiler_params=_params(("arbitrary",)),
        name="hyena_filter",
    )(zf, w1, b1, f1, w2, b2, f2, w3, decay)


S2_BLOCK = SUBLANES
BF16_ROWS = 2 * SUBLANES


def _split_bf16(x):
    hi = x.astype(BF16)
    return hi, (x - hi.astype(F32)).astype(BF16)


def _dot3(m_ref, x):
    x_hi, x_lo = _split_bf16(x)
    dot = functools.partial(jnp.dot, preferred_element_type=F32)
    return dot(m_ref[0], x_hi) + (dot(m_ref[0], x_lo) + dot(m_ref[1], x_hi))


def _dft_a_kernel(kz, kzp, f_ref, x_ref, o_ref):
    for j in range(S2_BLOCK):
        x = x_ref[0, :, j, :]
        if kzp != kz:
            x = jnp.concatenate([x, jnp.zeros((kzp - kz, x.shape[1]), F32)], axis=0)
        o_ref[0, :, j, :] = _dot3(f_ref, x)


def _dft_a(fmat, x):
    bsz, kz, n2, ch = x.shape
    _, m, kzp = fmat.shape
    return pl.pallas_call(
        functools.partial(_dft_a_kernel, kz, kzp),
        grid=(bsz, n2 // S2_BLOCK),
        in_specs=[
            pl.BlockSpec((2, m, kzp), lambda b, j: (0, 0, 0)),
            pl.BlockSpec((1, kz, S2_BLOCK, ch), lambda b, j: (b, 0, j, 0)),
        ],
        out_specs=pl.BlockSpec((1, m, S2_BLOCK, ch), lambda b, j: (b, 0, j, 0)),
        out_shape=jax.ShapeDtypeStruct((bsz, m, n2, ch), F32),
        compiler_params=_params(("parallel", "parallel")),
        name="dft_stage_a",
    )(fmat, x)


def _twiddle(ar, ai, twr, twi):
    return ar * twr - ai * twi, ar * twi + ai * twr


def _dft_b_filter_kernel(m_ref, twr_ref, twi_ref, a_ref, rs_ref, kf_ref):
    twr, twi = twr_ref[0], twi_ref[0]
    fwd = jnp.concatenate(_twiddle(a_ref[0, 0, 0], a_ref[0, 1, 0], twr, twi), axis=0)
    bwd = jnp.concatenate(_twiddle(a_ref[1, 0, 0], a_ref[1, 1, 0], twr, twi), axis=0)
    zf = _dot3(m_ref, fwd)
    zb = _dot3(m_ref, bwd)
    rs = rs_ref[...]
    kf_ref[0, 0] = (zf[:FFT_N2] + zb[:FFT_N2]) * rs
    kf_ref[1, 0] = (zf[FFT_N2:] - zb[FFT_N2:]) * rs


def _dft_b_filter(mc, twr, twi, a, rs):
    n1 = twr.shape[0]
    a5 = a.reshape(2, 2, n1, FFT_N2, D_HYENA)
    tw_spec = pl.BlockSpec((1, FFT_N2, 1), lambda f: (f, 0, 0))
    return pl.pallas_call(
        _dft_b_filter_kernel,
        grid=(n1,),
        in_specs=[
            pl.BlockSpec((2, 2 * FFT_N2, 2 * FFT_N2), lambda f: (0, 0, 0)),
            tw_spec, tw_spec,
            pl.BlockSpec((2, 2, 1, FFT_N2, D_HYENA), lambda f: (0, 0, f, 0, 0)),
            pl.BlockSpec((1, D_HYENA), lambda f: (0, 0)),
        ],
        out_specs=pl.BlockSpec((2, 1, FFT_N2, D_HYENA), lambda f: (0, f, 0, 0)),
        out_shape=jax.ShapeDtypeStruct((2, n1, FFT_N2, D_HYENA), F32),
        compiler_params=_params(("parallel",)),
        name="dft_filter_spectrum",
    )(mc, twr, twi, a5, rs)


def _dft_b_kernel(m_ref, mt_ref, twr_ref, twi_ref, a_ref, kf_ref, o_ref):
    twr, twi = twr_ref[0], twi_ref[0]
    x = jnp.concatenate(_twiddle(a_ref[0, 0, 0], a_ref[0, 1, 0], twr, twi), axis=0)
    z = _dot3(m_ref, x)
    y = jnp.concatenate(_twiddle(z[:FFT_N2], z[FFT_N2:], kf_ref[0, 0], kf_ref[1, 0]), axis=0)
    w = _dot3(mt_ref, y)
    o_ref[0, 0, 0], o_ref[0, 1, 0] = _twiddle(w[:FFT_N2], w[FFT_N2:], twr, -twi)


def _dft_b(mc, mct, twr, twi, a, kf):
    bsz = a.shape[0]
    n1 = twr.shape[0]
    a5 = a.reshape(bsz, 2, n1, FFT_N2, D_HYENA)
    mat_spec = pl.BlockSpec((2, 2 * FFT_N2, 2 * FFT_N2), lambda b, f: (0, 0, 0))
    tw_spec = pl.BlockSpec((1, FFT_N2, 1), lambda b, f: (f, 0, 0))
    out = pl.pallas_call(
        _dft_b_kernel,
        grid=(bsz, n1),
        in_specs=[
            mat_spec, mat_spec, tw_spec, tw_spec,
            pl.BlockSpec((1, 2, 1, FFT_N2, D_HYENA), lambda b, f: (b, 0, f, 0, 0)),
            pl.BlockSpec((2, 1, FFT_N2, D_HYENA), lambda b, f: (0, f, 0, 0)),
        ],
        out_specs=pl.BlockSpec((1, 2, 1, FFT_N2, D_HYENA), lambda b, f: (b, 0, f, 0, 0)),
        out_shape=jax.ShapeDtypeStruct((bsz, 2, n1, FFT_N2, D_HYENA), F32),
        compiler_params=_params(("parallel", "parallel")),
        name="dft_stage_b",
    )(mc, mct, twr, twi, a5, kf)
    return out.reshape(bsz, 2 * n1, FFT_N2, D_HYENA)


def _dft_c_kernel(kz, g_ref, w_ref, x0_ref, z_ref, d_ref, o_ref):
    for j in range(S2_BLOCK):
        y = jnp.dot(g_ref[...], w_ref[0, :, j, :], precision=HIGHEST, preferred_element_type=F32)
        o_ref[0, :, j, :] = x0_ref[0, :, j, :] * (y[:kz] + z_ref[0, :, j, :] * d_ref[...])


def _dft_c(gmat, w, x0, z, skip_d):
    bsz, kz, n2, ch = x0.shape
    kzp, m = gmat.shape
    seq_spec = pl.BlockSpec((1, kz, S2_BLOCK, ch), lambda b, j: (b, 0, j, 0))
    return pl.pallas_call(
        functools.partial(_dft_c_kernel, kz),
        grid=(bsz, n2 // S2_BLOCK),
        in_specs=[
            pl.BlockSpec((kzp, m), lambda b, j: (0, 0)),
            pl.BlockSpec((1, m, S2_BLOCK, ch), lambda b, j: (b, 0, j, 0)),
            seq_spec, seq_spec,
            pl.BlockSpec((1, ch), lambda b, j: (0, 0)),
        ],
        out_specs=seq_spec,
        out_shape=jax.ShapeDtypeStruct((bsz, kz, n2, ch), F32),
        compiler_params=_params(("parallel", "parallel")),
        name="dft_stage_c",
    )(gmat, w, x0, z, skip_d)


def _dft_tables(n1, kz, kzp):
    assert n1 % 2 == 0
    n2 = FFT_N2
    n = n1 * n2
    f1 = np.arange(_round_up(n1 // 2 + 1, SUBLANES))
    s1 = np.arange(kzp)
    ang = (2.0 * np.pi / n1) * ((f1[:, None] * s1[None, :]) % n1)
    live = (s1 < kz)[None, :]
    fa = np.concatenate([np.where(live, np.cos(ang), 0.0), np.where(live, -np.sin(ang), 0.0)], axis=0)
    weight = np.where((f1 == 0) | (f1 == n1 // 2), 1.0, np.where(f1 < n1 // 2, 2.0, 0.0))
    ga = np.concatenate([np.cos(ang).T * weight, -np.sin(ang).T * weight], axis=1)
    ga = ga * np.where(s1 < kz, 1.0 / n, 0.0)[:, None]
    idx = np.arange(n2)
    ang2 = (2.0 * np.pi / n2) * ((idx[:, None] * idx[None, :]) % n2)
    cr, ci = np.cos(ang2), -np.sin(ang2)
    mc = np.block([[cr, -ci], [ci, cr]])
    mct = np.block([[cr, ci], [-ci, cr]])
    ang3 = (2.0 * np.pi / n) * ((f1[:, None] * idx[None, :]) % n)
    twr, twi = np.cos(ang3)[:, :, None], -np.sin(ang3)[:, :, None]

    def hi_lo(mat):
        hi = mat.astype(BF16)
        lo = (mat - hi.astype(np.float64)).astype(BF16)
        return jnp.asarray(np.stack([hi, lo]))

    return hi_lo(fa), jnp.asarray(ga, F32), hi_lo(mc), hi_lo(mct), jnp.asarray(twr, F32), jnp.asarray(twi, F32)


def _rms(x, g):
    return x * lax.rsqrt(jnp.mean(x * x, axis=-1, keepdims=True) + RMS_EPS) * g


def _mla_proj_kernel(l_true, tm,
                     x_ref, cos_ref, sin_ref, wqa_ref, qn_ref, wq1_ref, wq2_ref, wc_ref, kn_ref,
                     wk1_ref, wk2_ref, wkk_ref, wkv_ref, q_ref, k_ref, v_ref):
    i = pl.program_id(1)
    x = x_ref[0].astype(BF16)
    cos = cos_ref[...]
    sin = sin_ref[...]
    lane = lax.broadcasted_iota(jnp.int32, (1, HEAD_SLOT), 1)

    cq = _rms(jnp.dot(x, wqa_ref[...], preferred_element_type=F32), qn_ref[...]).astype(BF16)
    q1 = jnp.dot(cq, wq1_ref[...], preferred_element_type=F32)
    q2 = jnp.dot(cq, wq2_ref[...], preferred_element_type=F32)
    q_one = jnp.where(lane == PAD_LANE, 1.0, 0.0)

    ckv = _rms(jnp.dot(x, wc_ref[...], preferred_element_type=F32), kn_ref[...]).astype(BF16)
    kn = jnp.dot(ckv, wkk_ref[...], preferred_element_type=F32)
    vv = jnp.dot(ckv, wkv_ref[...], preferred_element_type=F32)
    k1 = jnp.dot(x, wk1_ref[...], preferred_element_type=F32)
    k2 = jnp.dot(x, wk2_ref[...], preferred_element_type=F32)
    rows = i * tm + lax.broadcasted_iota(jnp.int32, (tm, 1), 0)
    k_pe = k1 * cos + k2 * sin + jnp.where((rows >= l_true) & (lane == PAD_LANE), PAD_SCORE, 0.0)
    v_one = jnp.where(lane == SUM_LANE, 1.0, 0.0)

    for h in range(N_HEADS):
        sl = slice(h * HEAD_SLOT, (h + 1) * HEAD_SLOT)
        q_ref[0, :, sl] = (q1[:, sl] * cos + q2[:, sl] * sin + q_one).astype(BF16)
        k_ref[0, :, sl] = (kn[:, sl] + k_pe).astype(BF16)
        v_ref[0, :, sl] = (vv[:, sl] + v_one).astype(BF16)


def _mla_proj(h, l_true, tm, cos_t, sin_t, wqa, qn, wq1, wq2, wc, kn, wk1, wk2, wkk, wkv):
    bsz, lp, _ = h.shape
    wide = N_HEADS * HEAD_SLOT
    full = lambda a: pl.BlockSpec(a.shape, lambda b, i: (0,) * a.ndim)
    out_sds = jax.ShapeDtypeStruct((bsz, lp, wide), BF16)
    weights = (wqa, qn, wq1, wq2, wc, kn, wk1, wk2, wkk, wkv)
    return pl.pallas_call(
        functools.partial(_mla_proj_kernel, l_true, tm),
        grid=(bsz, lp // tm),
        in_specs=[
            pl.BlockSpec((1, tm, D_MODEL), lambda b, i: (b, i, 0)),
            pl.BlockSpec((tm, HEAD_SLOT), lambda b, i: (i, 0)),
            pl.BlockSpec((tm, HEAD_SLOT), lambda b, i: (i, 0)),
        ] + [full(w) for w in weights],
        out_specs=[pl.BlockSpec((1, tm, wide), lambda b, i: (b, i, 0))] * 3,
        out_shape=[out_sds, out_sds, out_sds],
        compiler_params=_params(("parallel", "parallel")),
        name="mla_proj",
    )(h, cos_t, sin_t, *weights)


def _attn_kernel(tk, tr, q_ref, k_ref, v_ref, o_ref, s0_ref, s1_ref, m_ref, acc_ref):
    tq = q_ref.shape[1]
    nk = k_ref.shape[1] // tk
    s_refs = (s0_ref, s1_ref)
    row_blocks = [slice(r * tr, (r + 1) * tr) for r in range(tq // tr)]

    def scores(c, slot):
        k = k_ref[0, pl.ds(pl.multiple_of(c * tk, tk), tk), :]
        for rows in row_blocks:
            s_refs[slot][rows, :] = lax.dot_general(q_ref[0, rows, :], k, (((1,), (1,)), ((), ())),
                                                    preferred_element_type=F32)

    def softmax_pv(c, slot):
        v = v_ref[0, pl.ds(pl.multiple_of(c * tk, tk), tk), :]
        for rows in row_blocks:
            s = s_refs[slot][rows, :]
            m_old = m_ref[rows, :]
            m_new = jnp.maximum(m_old, jnp.max(s, axis=-1, keepdims=True))
            p = jnp.exp2(s - m_new).astype(BF16)
            acc_ref[rows, :] = (jnp.exp2(m_old - m_new) * acc_ref[rows, :]
                                + jnp.dot(p, v, preferred_element_type=F32))
            m_ref[rows, :] = m_new

    def step(c, slot):
        scores(c + 1, 1 - slot)
        softmax_pv(c, slot)

    m_ref[...] = jnp.full_like(m_ref, -jnp.inf)
    acc_ref[...] = jnp.zeros_like(acc_ref)
    scores(0, 0)

    def pair(j, carry):
        step(2 * j, 0)
        step(2 * j + 1, 1)
        return carry

    lax.fori_loop(0, (nk - 1) // 2, pair, 0)
    if (nk - 1) % 2:
        step(nk - 2, (nk - 2) % 2)
    softmax_pv(nk - 1, (nk - 1) % 2)
    acc = acc_ref[...]
    o_ref[0] = (acc / acc[:, SUM_LANE:SUM_LANE + 1]).astype(o_ref.dtype)


def _attn_batch_kernel(tk, tr, q_ref, k_ref, v_ref, o_ref, s0_ref, s1_ref, m_ref, acc_ref):
    nb, tq, _ = q_ref.shape
    nk = k_ref.shape[1] // tk
    s_refs = (s0_ref, s1_ref)
    row_blocks = [slice(r * tr, (r + 1) * tr) for r in range(tq // tr)]

    def scores(b, c, slot):
        k = k_ref[b, c * tk:(c + 1) * tk, :]
        for rows in row_blocks:
            s_refs[slot][rows, :] = lax.dot_general(q_ref[b, rows, :], k, (((1,), (1,)), ((), ())),
                                                    preferred_element_type=F32)

    def softmax_pv(b, c, slot):
        v = v_ref[b, c * tk:(c + 1) * tk, :]
        for rows in row_blocks:
            s = s_refs[slot][rows, :]
            m_old = m_ref[rows, :]
            m_new = jnp.maximum(m_old, jnp.max(s, axis=-1, keepdims=True))
            p = jnp.exp2(s - m_new).astype(BF16)
            acc_ref[rows, :] = (jnp.exp2(m_old - m_new) * acc_ref[rows, :]
                                + jnp.dot(p, v, preferred_element_type=F32))
            m_ref[rows, :] = m_new

    scores(0, 0, 0)

    def one_batch(b, carry):
        m_ref[...] = jnp.full_like(m_ref, -jnp.inf)
        acc_ref[...] = jnp.zeros_like(acc_ref)
        for c in range(nk):
            if c + 1 < nk:
                scores(b, c + 1, (c + 1) % 2)
            else:
                scores(jnp.minimum(b + 1, nb - 1), 0, 0)
            softmax_pv(b, c, c % 2)
        acc = acc_ref[...]
        o_ref[b] = (acc / acc[:, SUM_LANE:SUM_LANE + 1]).astype(o_ref.dtype)
        return carry

    lax.fori_loop(0, nb, one_batch, 0)


def _attention(q, k, v, tq, tk, tr):
    bsz, lp, wide = q.shape
    scratch = [pltpu.VMEM((tq, tk), F32), pltpu.VMEM((tq, tk), F32),
               pltpu.VMEM((tq, 1), F32), pltpu.VMEM((tq, HEAD_SLOT), F32)]
    if bsz > 1 and (lp // tk) % 2 == 0:
        return pl.pallas_call(
            functools.partial(_attn_batch_kernel, tk, _divisor_tile(tq, 640, 2 * SUBLANES)),
            grid=(N_HEADS, lp // tq),
            in_specs=[
                pl.BlockSpec((bsz, tq, HEAD_SLOT), lambda h, i: (0, i, h)),
                pl.BlockSpec((bsz, lp, HEAD_SLOT), lambda h, i: (0, 0, h)),
                pl.BlockSpec((bsz, lp, HEAD_SLOT), lambda h, i: (0, 0, h)),
            ],
            out_specs=pl.BlockSpec((bsz, tq, HEAD_SLOT), lambda h, i: (0, i, h)),
            out_shape=jax.ShapeDtypeStruct((bsz, lp, wide), BF16),
            scratch_shapes=scratch,
            compiler_params=_params(("parallel", "arbitrary")),
            name="attention_batched",
        )(q, k, v)
    return pl.pallas_call(
        functools.partial(_attn_kernel, tk, tr),
        grid=(bsz, N_HEADS, lp // tq),
        in_specs=[
            pl.BlockSpec((1, tq, HEAD_SLOT), lambda b, h, i: (b, i, h)),
            pl.BlockSpec((1, lp, HEAD_SLOT), lambda b, h, i: (b, 0, h)),
            pl.BlockSpec((1, lp, HEAD_SLOT), lambda b, h, i: (b, 0, h)),
        ],
        out_specs=pl.BlockSpec((1, tq, HEAD_SLOT), lambda b, h, i: (b, i, h)),
        out_shape=jax.ShapeDtypeStruct((bsz, lp, wide), BF16),
        scratch_shapes=scratch,
        compiler_params=_params(("parallel", "parallel", "arbitrary")),
        name="attention",
    )(q, k, v)


def _slot_cols(w, per_head, pieces):
    k = w.shape[0]
    wh = w.reshape(k, N_HEADS, per_head)
    out = jnp.zeros((k, N_HEADS, HEAD_SLOT), w.dtype)
    for dst, src, width, sign in pieces:
        out = out.at[:, :, dst:dst + width].set(sign * wh[:, :, src:src + width])
    return out.reshape(k, N_HEADS * HEAD_SLOT)


def _mla_weights(wq_b, wkv_a, wkv_b, wo):
    half = QK_ROPE // 2
    qk = QK_NOPE + QK_ROPE
    q_scale = (qk ** -0.5) * math.log2(math.e)
    wq1 = _slot_cols(wq_b, qk, [(0, 0, QK_NOPE, 1.0), (QK_NOPE, QK_NOPE, QK_ROPE, 1.0)]) * q_scale
    wq2 = _slot_cols(wq_b, qk, [(QK_NOPE, QK_NOPE + half, half, -1.0),
                                (QK_NOPE + half, QK_NOPE, half, 1.0)]) * q_scale
    wc = wkv_a[:, :KV_RANK]
    pe = wkv_a[:, KV_RANK:]
    zeros = lambda n: jnp.zeros((D_MODEL, n), wkv_a.dtype)
    tail = HEAD_SLOT - QK_NOPE - QK_ROPE
    wk1 = jnp.concatenate([zeros(QK_NOPE), pe, zeros(tail)], axis=1)
    wk2 = jnp.concatenate([zeros(QK_NOPE), -pe[:, half:], pe[:, :half], zeros(tail)], axis=1)
    wkk = _slot_cols(wkv_b, QK_NOPE + V_HEAD, [(0, 0, QK_NOPE, 1.0)])
    wkv = _slot_cols(wkv_b, QK_NOPE + V_HEAD, [(0, QK_NOPE, V_HEAD, 1.0)])
    wo_slot = jnp.zeros((N_HEADS, HEAD_SLOT, D_MODEL), wo.dtype)
    wo_slot = wo_slot.at[:, :V_HEAD, :].set(wo.reshape(N_HEADS, V_HEAD, D_MODEL))
    wo_slot = wo_slot.reshape(N_HEADS * HEAD_SLOT, D_MODEL)
    cast = lambda a: a.astype(BF16)
    return tuple(map(cast, (wq1, wq2, wc, wk1, wk2, wkk, wkv, wo_slot)))


def _rope_slot_tables(lp):
    pos = jnp.arange(lp, dtype=F32)
    inv = 1.0 / (ROPE_THETA ** (jnp.arange(0, QK_ROPE, 2, dtype=F32) / QK_ROPE))
    ang = pos[:, None] * inv[None, :]
    cos, sin = jnp.cos(ang), jnp.sin(ang)
    tail = HEAD_SLOT - QK_NOPE - QK_ROPE
    cos_t = jnp.concatenate([jnp.ones((lp, QK_NOPE), F32), cos, cos, jnp.zeros((lp, tail), F32)], axis=1)
    sin_t = jnp.concatenate([jnp.zeros((lp, QK_NOPE), F32), sin, sin, jnp.zeros((lp, tail), F32)], axis=1)
    return cos_t, sin_t


def _filter_features(l_true, lp):
    t = jnp.arange(lp, dtype=F32) / max(l_true - 1, 1)
    bands = (FILTER_EMB - 1) // 2
    freqs = jnp.linspace(1e-4, bands - 1, bands, dtype=F32)
    w = 2.0 * math.pi * jnp.arange(lp, dtype=F32) / l_true
    ang = w[:, None] * freqs[None, :]
    z = jnp.concatenate([t[:, None], jnp.cos(ang), -jnp.sin(ang)], axis=-1)
    return jnp.pad(z, ((0, 0), (0, FILTER_EMB_PAD - FILTER_EMB)))


def _divisor_tile(total, target, multiple):
    best = None
    for d in range(multiple, min(total, target) + 1, multiple):
        if total % d == 0:
            best = d
    assert best is not None, (total, target, multiple)
    return best


SHORT_SEQ_MAX = 2304
MXU_COLS = 256


def _tiling(bsz, l_true):
    lp = _round_up(l_true, LANES if l_true <= SHORT_SEQ_MAX else MXU_COLS)
    tk = _divisor_tile(lp, 1280, LANES // 2)
    tq = _divisor_tile(lp, 1280, 2 * SUBLANES)
    return dict(
        lp=lp, tk=tk,
        tl=_divisor_tile(lp, 640, HALO),
        tq=tq, tr=_divisor_tile(tq, 256, 2 * SUBLANES),
        tm=_divisor_tile(bsz * lp, 1088, 2 * SUBLANES),
        tmlp=_divisor_tile(bsz * lp, 640, 2 * SUBLANES),
    )


def _trunk(x, p):
    bsz, seq, _ = x.shape
    l_true = seq + N_META
    cfg = _tiling(bsz, l_true)
    lp = cfg["lp"]
    rows = bsz * lp

    meta = jnp.broadcast_to(p["meta_tokens"][None], (bsz, N_META, D_MODEL))
    h = jnp.concatenate([meta, x, jnp.zeros((bsz, lp - l_true, D_MODEL), F32)], axis=1)
    h = h.reshape(rows, D_MODEL)

    proj = _mm_bias(h, p["ev_w_in"], p["ev_b_in"], cfg["tm"]).reshape(bsz, lp, D_IN_EVEN)
    x0, z, y_b = _conv_mix(proj, l_true, cfg["tl"], p["ev_short_w"], p["ev_short_b"],
                           p["cf_dw_w"], p["cf_dw_b"], p["cf_ln_g"], p["cf_ln_b"])

    kz = lp // FFT_N2
    kzp = _round_up(kz, BF16_ROWS)
    n1 = _round_up(-(-(2 * l_true - 1) // FFT_N2), SUBLANES)
    fa, ga, mc, mct, twr, twi = _dft_tables(n1, kz, kzp)
    seq4 = lambda a: a.reshape(a.shape[0], kz, FFT_N2, D_HYENA)

    zf = _filter_features(l_true, lp)
    taps, sumsq = _hyena_filter(zf, l_true, cfg["tl"], p["hy_w1"], p["hy_b1"], p["hy_freq1"], p["hy_w2"],
                                p["hy_b2"], p["hy_freq2"], p["hy_w3"], p["hy_decay"])
    rs = lax.rsqrt(sumsq)
    kf = _dft_b_filter(mc, twr, twi, _dft_a(fa, seq4(taps)), rs)

    wb = _dft_b(mc, mct, twr, twi, _dft_a(fa, seq4(z)), kf)
    y_a = _dft_c(ga, wb, seq4(x0), seq4(z), p["hy_skip_d"]).reshape(rows, D_HYENA)

    h = _mm_res_ln([y_a, y_b.reshape(rows, D_CONF)], [p["ev_w_out_a"], p["ev_w_out_b"]], p["ev_b_out"],
                   h, p["ln1_g"][0], p["ln1_b"][0], cfg["tm"], "out_proj_even")
    h = _mlp(h, p["mlp_w1"][0], p["mlp_w2"][0], p["ln2_g"][0], p["ln2_b"][0], cfg["tmlp"])

    cos_t, sin_t = _rope_slot_tables(lp)
    q, k, v = _mla_proj(h.reshape(bsz, lp, D_MODEL), l_true, cfg["tl"], cos_t, sin_t,
                        p["mla_wq_a"], p["mla_q_norm"], p["wq1"], p["wq2"], p["wc"], p["mla_kv_norm"],
                        p["wk1"], p["wk2"], p["wkk"], p["wkv"])
    o = _attention(q, k, v, cfg["tq"], cfg["tk"], cfg["tr"]).reshape(rows, N_HEADS * HEAD_SLOT)
    h = _mm_res_ln([o], [p["wo_slot"]], jnp.zeros((1, D_MODEL), F32), h, p["ln1_g"][1], p["ln1_b"][1],
                   cfg["tm"], "out_proj_odd")
    h = _mlp(h, p["mlp_w1"][1], p["mlp_w2"][1], p["ln2_g"][1], p["ln2_b"][1], cfg["tmlp"])

    return h.reshape(bsz, lp, D_MODEL)[:, N_META:l_true]


def kernel(x_prompt, x_sample, meta_tokens, ev_w_in, ev_b_in, ev_short_w, ev_short_b, hy_w1, hy_b1, hy_freq1,
           hy_w2, hy_b2, hy_freq2, hy_w3, hy_decay, hy_skip_d, cf_dw_w, cf_dw_b, cf_ln_g, cf_ln_b, ev_w_out,
           ev_b_out, mla_wq_a, mla_q_norm, mla_wq_b, mla_wkv_a, mla_kv_norm, mla_wkv_b, mla_wo, ln1_g, ln1_b,
           mlp_w1, mlp_w2, ln2_g, ln2_b):
    row = lambda a: a.reshape(1, -1)
    wq1, wq2, wc, wk1, wk2, wkk, wkv, wo_slot = _mla_weights(mla_wq_b[0], mla_wkv_a[0], mla_wkv_b[0], mla_wo[0])
    p = dict(
        meta_tokens=meta_tokens,
        ev_w_in=ev_w_in[0].astype(BF16), ev_b_in=row(ev_b_in[0]),
        ev_short_w=ev_short_w[0], ev_short_b=row(ev_short_b[0]),
        hy_w1=jnp.pad(hy_w1[0], ((0, 0), (0, FILTER_EMB_PAD - FILTER_EMB), (0, 0))),
        hy_b1=hy_b1[0][:, None, :], hy_freq1=hy_freq1[0][:, None, :],
        hy_w2=hy_w2[0], hy_b2=hy_b2[0][:, None, :], hy_freq2=hy_freq2[0][:, None, :],
        hy_w3=hy_w3[0], hy_decay=hy_decay[0][:, None, :], hy_skip_d=row(hy_skip_d[0]),
        cf_dw_w=cf_dw_w[0], cf_dw_b=row(cf_dw_b[0]), cf_ln_g=row(cf_ln_g[0]), cf_ln_b=row(cf_ln_b[0]),
        ev_w_out_a=ev_w_out[0, :D_HYENA].astype(BF16), ev_w_out_b=ev_w_out[0, D_HYENA:].astype(BF16),
        ev_b_out=row(ev_b_out[0]),
        mla_wq_a=mla_wq_a[0].astype(BF16), mla_q_norm=row(mla_q_norm[0]), mla_kv_norm=row(mla_kv_norm[0]),
        wq1=wq1, wq2=wq2, wc=wc, wk1=wk1, wk2=wk2, wkk=wkk, wkv=wkv, wo_slot=wo_slot,
        ln1_g=ln1_g[:, None, :], ln1_b=ln1_b[:, None, :], ln2_g=ln2_g[:, None, :], ln2_b=ln2_b[:, None, :],
        mlp_w1=mlp_w1.astype(BF16), mlp_w2=mlp_w2.astype(BF16),
    )
    return (_trunk(x_prompt, p), _trunk(x_sample, p))
```

```python
import functools
import math

import jax
import jax.numpy as jnp
import numpy as np
from jax import lax
from jax.experimental import pallas as pl
from jax.experimental.pallas import tpu as pltpu

F32 = jnp.float32
BF16 = jnp.bfloat16

D_MODEL = 1024
N_META = 16
D_HYENA = 512
D_CONF = 512
D_IN_EVEN = 3 * D_HYENA + 2 * D_CONF
CONF_K = 31
FILTER_EMB = 33
FILTER_EMB_PAD = 40
FILTER_HIDDEN = 64
N_HEADS = 16
QK_NOPE = 64
QK_ROPE = 32
V_HEAD = 64
Q_RANK = 384
KV_RANK = 256
ROPE_THETA = 10000.0
D_FF = 4096
DEPTH = 2
DN_ALPHA = (2 * DEPTH) ** 0.25
LN_EPS = 1e-5
RMS_EPS = 1e-6

LANES = 128
SUBLANES = 8
HEAD_SLOT = LANES
HALO = 16
VMEM_LIMIT = 56 * 1024 * 1024

PAD_LANE = QK_NOPE + QK_ROPE
SUM_LANE = V_HEAD
PAD_SCORE = -1e30
FFT_N2 = 128

HIGHEST = lax.Precision.HIGHEST


def _params(sem, vmem=VMEM_LIMIT):
    return pltpu.CompilerParams(dimension_semantics=sem, vmem_limit_bytes=vmem)


def _round_up(x, m):
    return -(-x // m) * m


def _layer_norm(x, g, b):
    mu = jnp.mean(x, axis=-1, keepdims=True)
    xc = x - mu
    var = jnp.mean(xc * xc, axis=-1, keepdims=True)
    return xc * lax.rsqrt(var + LN_EPS) * g + b


def _mm_bias_kernel(x_ref, w_ref, b_ref, o_ref):
    acc = jnp.dot(x_ref[...].astype(BF16), w_ref[...], preferred_element_type=F32)
    o_ref[...] = (acc + b_ref[...]).astype(o_ref.dtype)


def _mm_bias(x, w, b, tm, out_dtype=F32):
    rows, k = x.shape
    n = w.shape[1]
    return pl.pallas_call(
        _mm_bias_kernel,
        grid=(pl.cdiv(rows, tm),),
        in_specs=[
            pl.BlockSpec((tm, k), lambda i: (i, 0)),
            pl.BlockSpec((k, n), lambda i: (0, 0)),
            pl.BlockSpec((1, n), lambda i: (0, 0)),
        ],
        out_specs=pl.BlockSpec((tm, n), lambda i: (i, 0)),
        out_shape=jax.ShapeDtypeStruct((rows, n), out_dtype),
        compiler_params=_params(("parallel",)),
        name="in_proj",
    )(x, w, b)


def _mm_res_ln_kernel(n_in, *refs):
    a_refs = refs[:n_in]
    w_refs = refs[n_in:2 * n_in]
    bias_ref, res_ref, g_ref, b_ref, o_ref = refs[2 * n_in:]
    acc = bias_ref[...] + DN_ALPHA * res_ref[...]
    for a_ref, w_ref in zip(a_refs, w_refs):
        acc = acc + jnp.dot(a_ref[...].astype(BF16), w_ref[...], preferred_element_type=F32)
    o_ref[...] = _layer_norm(acc, g_ref[...], b_ref[...])


def _mm_res_ln(a_list, w_list, bias, res, g, b, tm, name):
    rows = res.shape[0]
    n_in = len(a_list)
    in_specs = [pl.BlockSpec((tm, a.shape[1]), lambda i: (i, 0)) for a in a_list]
    in_specs += [pl.BlockSpec(w.shape, lambda i: (0, 0)) for w in w_list]
    in_specs += [
        pl.BlockSpec((1, D_MODEL), lambda i: (0, 0)),
        pl.BlockSpec((tm, D_MODEL), lambda i: (i, 0)),
        pl.BlockSpec((1, D_MODEL), lambda i: (0, 0)),
        pl.BlockSpec((1, D_MODEL), lambda i: (0, 0)),
    ]
    return pl.pallas_call(
        functools.partial(_mm_res_ln_kernel, n_in),
        grid=(pl.cdiv(rows, tm),),
        in_specs=in_specs,
        out_specs=pl.BlockSpec((tm, D_MODEL), lambda i: (i, 0)),
        out_shape=jax.ShapeDtypeStruct((rows, D_MODEL), F32),
        compiler_params=_params(("parallel",)),
        name=name,
    )(*a_list, *w_list, bias, res, g, b)


MLP_HIDDEN_CHUNK = 1024


def _mlp_kernel(x_ref, w1_ref, w2_ref, g_ref, b_ref, o_ref):
    o_ref[...] = _mlp_rows(x_ref[...], w1_ref, w2_ref, g_ref, b_ref)


def _mlp_final_kernel(x_ref, next_ref, w1_ref, w2_ref, g_ref, b_ref, o_ref):
    x = jnp.concatenate([x_ref[0, N_META:, :], next_ref[0]], axis=0)
    o_ref[0] = _mlp_rows(x, w1_ref, w2_ref, g_ref, b_ref)


def _mlp_rows(x, w1_ref, w2_ref, g_ref, b_ref):
    xb = x.astype(BF16)
    acc = DN_ALPHA * x
    for c0 in range(0, D_FF, MLP_HIDDEN_CHUNK):
        hid = jnp.dot(xb, w1_ref[:, c0:c0 + MLP_HIDDEN_CHUNK], preferred_element_type=F32)
        hid = jnp.maximum(hid, 0.0)
        hid = (hid * hid).astype(BF16)
        acc = acc + jnp.dot(hid, w2_ref[c0:c0 + MLP_HIDDEN_CHUNK, :], preferred_element_type=F32)
    return _layer_norm(acc, g_ref[...], b_ref[...])


def _mlp_final(x, seq, w1, w2, g, b, tm):
    bsz = x.shape[0]
    full = lambda shape: pl.BlockSpec(shape, lambda bi, i: (0, 0))
    return pl.pallas_call(
        _mlp_final_kernel,
        grid=(bsz, seq // tm),
        in_specs=[
            pl.BlockSpec((1, tm, D_MODEL), lambda bi, i: (bi, i, 0)),
            pl.BlockSpec((1, N_META, D_MODEL), lambda bi, i: (bi, (i + 1) * (tm // N_META), 0)),
            full((D_MODEL, D_FF)), full((D_FF, D_MODEL)), full((1, D_MODEL)), full((1, D_MODEL)),
        ],
        out_specs=pl.BlockSpec((1, tm, D_MODEL), lambda bi, i: (bi, i, 0)),
        out_shape=jax.ShapeDtypeStruct((bsz, seq, D_MODEL), F32),
        compiler_params=_params(("parallel", "parallel")),
        name="mlp_final",
    )(x, x, w1, w2, g, b)


def _mlp(x, w1, w2, g, b, tm):
    rows = x.shape[0]
    return pl.pallas_call(
        _mlp_kernel,
        grid=(pl.cdiv(rows, tm),),
        in_specs=[
            pl.BlockSpec((tm, D_MODEL), lambda i: (i, 0)),
            pl.BlockSpec((D_MODEL, D_FF), lambda i: (0, 0)),
            pl.BlockSpec((D_FF, D_MODEL), lambda i: (0, 0)),
            pl.BlockSpec((1, D_MODEL), lambda i: (0, 0)),
            pl.BlockSpec((1, D_MODEL), lambda i: (0, 0)),
        ],
        out_specs=pl.BlockSpec((tm, D_MODEL), lambda i: (i, 0)),
        out_shape=jax.ShapeDtypeStruct((rows, D_MODEL), F32),
        compiler_params=_params(("parallel",)),
        name="mlp",
    )(x, w1, w2, g, b)


def _conv_mix_kernel(l_true, tl, rc,
                     main_ref, prev_ref, next_ref, sw_ref, sb_ref, dw_ref, db_ref, g_ref, b_ref,
                     x0_ref, z_ref, yb_ref, ext_ref, glu_ref):
    i = pl.program_id(1)
    row0 = i * tl - HALO

    def valid_rows(start, n):
        g = start + lax.broadcasted_iota(jnp.int32, (n, 1), 0)
        return (g >= 0) & (g < l_true)

    ext_ref[0:HALO, :] = jnp.where(valid_rows(row0, HALO), prev_ref[0], 0.0)
    ext_ref[HALO:HALO + tl, :] = jnp.where(valid_rows(row0 + HALO, tl), main_ref[0], 0.0)
    ext_ref[HALO + tl:, :] = jnp.where(valid_rows(row0 + HALO + tl, HALO), next_ref[0], 0.0)

    a = ext_ref[:, 3 * D_HYENA:3 * D_HYENA + D_CONF]
    gate = ext_ref[:, 3 * D_HYENA + D_CONF:]
    glu_ref[...] = a * jax.nn.sigmoid(gate)

    sw = sw_ref[...]
    sb = sb_ref[...]
    dw = dw_ref[...]
    half = (CONF_K - 1) // 2

    def chunk(c, carry):
        r0 = pl.multiple_of(c * rc, SUBLANES)
        rows_ok = valid_rows(i * tl + r0, rc)

        win = ext_ref[pl.ds(r0 + HALO - SUBLANES, rc + 2 * SUBLANES), 0:3 * D_HYENA]
        hy = sb
        for k in range(3):
            off = SUBLANES + k - 1
            hy = hy + sw[k:k + 1, :] * win[off:off + rc, :]
        x0_ref[0, pl.ds(r0, rc), :] = hy[:, 0:D_HYENA]
        zz = hy[:, D_HYENA:2 * D_HYENA] * hy[:, 2 * D_HYENA:]
        z_ref[0, pl.ds(r0, rc), :] = jnp.where(rows_ok, zz, 0.0)

        gwin = glu_ref[pl.ds(r0, rc + 2 * HALO), :]
        acc = db_ref[...]
        for j in range(SUBLANES):
            part = None
            for k in range(CONF_K):
                off = HALO + k - half
                if off % SUBLANES == j:
                    base = off - j
                    term = dw[k:k + 1, :] * gwin[base:base + rc + SUBLANES, :]
                    part = term if part is None else part + term
            acc = acc + part[j:j + rc, :]
        y = _layer_norm(acc, g_ref[...], b_ref[...])
        yb_ref[0, pl.ds(r0, rc), :] = y * jax.nn.sigmoid(y)
        return carry

    lax.fori_loop(0, tl // rc, chunk, 0)


def _conv_mix(proj, l_true, tl, short_w, short_b, dw_w, dw_b, ln_g, ln_b):
    bsz, lp, _ = proj.shape
    nt = lp // tl
    hb = tl // HALO
    rc = _divisor_tile(tl, 64, SUBLANES)
    out_sds = jax.ShapeDtypeStruct((bsz, lp, D_HYENA), F32)
    row_spec = lambda shape: pl.BlockSpec(shape, lambda b, i: (0, 0))
    return pl.pallas_call(
        functools.partial(_conv_mix_kernel, l_true, tl, rc),
        grid=(bsz, nt),
        in_specs=[
            pl.BlockSpec((1, tl, D_IN_EVEN), lambda b, i: (b, i, 0)),
            pl.BlockSpec((1, HALO, D_IN_EVEN), lambda b, i: (b, jnp.maximum(i * hb - 1, 0), 0)),
            pl.BlockSpec((1, HALO, D_IN_EVEN), lambda b, i: (b, jnp.minimum((i + 1) * hb, nt * hb - 1), 0)),
            row_spec((3, 3 * D_HYENA)),
            row_spec((1, 3 * D_HYENA)),
            row_spec((CONF_K, D_CONF)),
            row_spec((1, D_CONF)),
            row_spec((1, D_CONF)),
            row_spec((1, D_CONF)),
        ],
        out_specs=[pl.BlockSpec((1, tl, D_HYENA), lambda b, i: (b, i, 0))] * 3,
        out_shape=[out_sds, out_sds, out_sds],
        scratch_shapes=[
            pltpu.VMEM((tl + 2 * HALO, D_IN_EVEN), F32),
            pltpu.VMEM((tl + 2 * HALO, D_CONF), F32),
        ],
        compiler_params=_params(("parallel", "parallel")),
        name="conv_mix",
    )(proj, proj, proj, short_w, short_b, dw_w, dw_b, ln_g, ln_b)


def _filter_kernel(l_true, tl, zf_ref, w1_ref, b1_ref, f1_ref, w2_ref, b2_ref, f2_ref, w3_ref, dec_ref,
                   h_ref, ss_ref):
    i = pl.program_id(0)

    @pl.when(i == 0)
    def _():
        ss_ref[...] = jnp.zeros_like(ss_ref)

    zf = zf_ref[...]
    t = zf[:, 0:1]
    g = i * tl + lax.broadcasted_iota(jnp.int32, (tl, 1), 0)
    total = jnp.zeros((1, D_HYENA), F32)
    for d in range(2):
        a = jnp.dot(zf, w1_ref[d], precision=HIGHEST, preferred_element_type=F32) + b1_ref[d]
        a = jnp.sin(f1_ref[d] * a)
        a = jnp.dot(a, w2_ref[d], precision=HIGHEST, preferred_element_type=F32) + b2_ref[d]
        a = jnp.sin(f2_ref[d] * a)
        hh = jnp.dot(a, w3_ref[d], precision=HIGHEST, preferred_element_type=F32)
        hh = hh * jnp.exp(-t * dec_ref[d])
        keep = (g < l_true) & (g >= d)
        hh = jnp.where(keep, hh, 0.0)
        h_ref[d] = hh
        total = total + jnp.sum(hh * hh, axis=0, keepdims=True)
    ss_ref[...] += total


def _hyena_filter(zf, l_true, tl, w1, b1, f1, w2, b2, f2, w3, decay):
    lp = zf.shape[0]
    full = lambda shape: pl.BlockSpec(shape, lambda i: (0,) * len(shape))
    return pl.pallas_call(
        functools.partial(_filter_kernel, l_true, tl),
        grid=(lp // tl,),
        in_specs=[
            pl.BlockSpec((tl, FILTER_EMB_PAD), lambda i: (i, 0)),
            full((2, FILTER_EMB_PAD, FILTER_HIDDEN)),
            full((2, 1, FILTER_HIDDEN)),
            full((2, 1, FILTER_HIDDEN)),
            full((2, FILTER_HIDDEN, FILTER_HIDDEN)),
            full((2, 1, FILTER_HIDDEN)),
            full((2, 1, FILTER_HIDDEN)),
            full((2, FILTER_HIDDEN, D_HYENA)),
            full((2, 1, D_HYENA)),
        ],
        out_specs=[
            pl.BlockSpec((2, tl, D_HYENA), lambda i: (0, i, 0)),
            pl.BlockSpec((1, D_HYENA), lambda i: (0, 0)),
        ],
        out_shape=[
            jax.ShapeDtypeStruct((2, lp, D_HYENA), F32),
            jax.ShapeDtypeStruct((1, D_HYENA), F32),
        ],
        compiler_params=_params(("arbitrary",)),
        name="hyena_filter",
    )(zf, w1, b1, f1, w2, b2, f2, w3, decay)


S2_BLOCK = SUBLANES
BF16_ROWS = 2 * SUBLANES


def _split_bf16(x):
    hi = x.astype(BF16)
    return hi, (x - hi.astype(F32)).astype(BF16)


def _dot3(m_ref, x):
    x_hi, x_lo = _split_bf16(x)
    dot = functools.partial(jnp.dot, preferred_element_type=F32)
    return dot(m_ref[0], x_hi) + (dot(m_ref[0], x_lo) + dot(m_ref[1], x_hi))


def _dft_a_kernel(kz, kzp, f_ref, x_ref, o_ref):
    for j in range(S2_BLOCK):
        x = x_ref[0, :, j, :]
        if kzp != kz:
            x = jnp.concatenate([x, jnp.zeros((kzp - kz, x.shape[1]), F32)], axis=0)
        o_ref[0, :, j, :] = _dot3(f_ref, x)


def _dft_a(fmat, x):
    bsz, kz, n2, ch = x.shape
    _, m, kzp = fmat.shape
    return pl.pallas_call(
        functools.partial(_dft_a_kernel, kz, kzp),
        grid=(bsz, n2 // S2_BLOCK),
        in_specs=[
            pl.BlockSpec((2, m, kzp), lambda b, j: (0, 0, 0)),
            pl.BlockSpec((1, kz, S2_BLOCK, ch), lambda b, j: (b, 0, j, 0)),
        ],
        out_specs=pl.BlockSpec((1, m, S2_BLOCK, ch), lambda b, j: (b, 0, j, 0)),
        out_shape=jax.ShapeDtypeStruct((bsz, m, n2, ch), F32),
        compiler_params=_params(("parallel", "parallel")),
        name="dft_stage_a",
    )(fmat, x)


def _twiddle(ar, ai, twr, twi):
    return ar * twr - ai * twi, ar * twi + ai * twr


def _dft_b_filter_kernel(m_ref, twr_ref, twi_ref, a_ref, rs_ref, kf_ref):
    rs = rs_ref[...]
    for i in range(twr_ref.shape[0]):
        twr, twi = twr_ref[i], twi_ref[i]
        fwd = jnp.concatenate(_twiddle(a_ref[0, 0, i], a_ref[0, 1, i], twr, twi), axis=0)
        bwd = jnp.concatenate(_twiddle(a_ref[1, 0, i], a_ref[1, 1, i], twr, twi), axis=0)
        zf = _dot3(m_ref, fwd)
        zb = _dot3(m_ref, bwd)
        kf_ref[0, i] = (zf[:FFT_N2] + zb[:FFT_N2]) * rs
        kf_ref[1, i] = (zf[FFT_N2:] - zb[FFT_N2:]) * rs


def _dft_b_filter(mc, twr, twi, a, rs):
    n1 = twr.shape[0]
    fb = _divisor_tile(n1, SLABS_PER_STEP, 1)
    a5 = a.reshape(2, 2, n1, FFT_N2, D_HYENA)
    tw_spec = pl.BlockSpec((fb, FFT_N2, 1), lambda f: (f, 0, 0))
    return pl.pallas_call(
        _dft_b_filter_kernel,
        grid=(n1 // fb,),
        in_specs=[
            pl.BlockSpec((2, 2 * FFT_N2, 2 * FFT_N2), lambda f: (0, 0, 0)),
            tw_spec, tw_spec,
            pl.BlockSpec((2, 2, fb, FFT_N2, D_HYENA), lambda f: (0, 0, f, 0, 0)),
            pl.BlockSpec((1, D_HYENA), lambda f: (0, 0)),
        ],
        out_specs=pl.BlockSpec((2, fb, FFT_N2, D_HYENA), lambda f: (0, f, 0, 0)),
        out_shape=jax.ShapeDtypeStruct((2, n1, FFT_N2, D_HYENA), F32),
        compiler_params=_params(("parallel",)),
        name="dft_filter_spectrum",
    )(mc, twr, twi, a5, rs)


def _dft_b_kernel(m_ref, mt_ref, twr_ref, twi_ref, a_ref, kf_ref, o_ref):
    for i in range(twr_ref.shape[0]):
        twr, twi = twr_ref[i], twi_ref[i]
        x = jnp.concatenate(_twiddle(a_ref[0, 0, i], a_ref[0, 1, i], twr, twi), axis=0)
        z = _dot3(m_ref, x)
        y = jnp.concatenate(_twiddle(z[:FFT_N2], z[FFT_N2:], kf_ref[0, i], kf_ref[1, i]), axis=0)
        w = _dot3(mt_ref, y)
        o_ref[0, 0, i], o_ref[0, 1, i] = _twiddle(w[:FFT_N2], w[FFT_N2:], twr, -twi)


SLABS_PER_STEP = 4


def _dft_b(mc, mct, twr, twi, a, kf):
    bsz = a.shape[0]
    n1 = twr.shape[0]
    fb = _divisor_tile(n1, SLABS_PER_STEP, 1)
    a5 = a.reshape(bsz, 2, n1, FFT_N2, D_HYENA)
    mat_spec = pl.BlockSpec((2, 2 * FFT_N2, 2 * FFT_N2), lambda b, f: (0, 0, 0))
    tw_spec = pl.BlockSpec((fb, FFT_N2, 1), lambda b, f: (f, 0, 0))
    out = pl.pallas_call(
        _dft_b_kernel,
        grid=(bsz, n1 // fb),
        in_specs=[
            mat_spec, mat_spec, tw_spec, tw_spec,
            pl.BlockSpec((1, 2, fb, FFT_N2, D_HYENA), lambda b, f: (b, 0, f, 0, 0)),
            pl.BlockSpec((2, fb, FFT_N2, D_HYENA), lambda b, f: (0, f, 0, 0)),
        ],
        out_specs=pl.BlockSpec((1, 2, fb, FFT_N2, D_HYENA), lambda b, f: (b, 0, f, 0, 0)),
        out_shape=jax.ShapeDtypeStruct((bsz, 2, n1, FFT_N2, D_HYENA), F32),
        compiler_params=_params(("parallel", "parallel")),
        name="dft_stage_b",
    )(mc, mct, twr, twi, a5, kf)
    return out.reshape(bsz, 2 * n1, FFT_N2, D_HYENA)


def _dft_c_kernel(kz, g_ref, w_ref, x0_ref, z_ref, d_ref, o_ref):
    for j in range(S2_BLOCK):
        y = jnp.dot(g_ref[...], w_ref[0, :, j, :], precision=HIGHEST, preferred_element_type=F32)
        o_ref[0, :, j, :] = x0_ref[0, :, j, :] * (y[:kz] + z_ref[0, :, j, :] * d_ref[...])


def _dft_c(gmat, w, x0, z, skip_d):
    bsz, kz, n2, ch = x0.shape
    kzp, m = gmat.shape
    seq_spec = pl.BlockSpec((1, kz, S2_BLOCK, ch), lambda b, j: (b, 0, j, 0))
    return pl.pallas_call(
        functools.partial(_dft_c_kernel, kz),
        grid=(bsz, n2 // S2_BLOCK),
        in_specs=[
            pl.BlockSpec((kzp, m), lambda b, j: (0, 0)),
            pl.BlockSpec((1, m, S2_BLOCK, ch), lambda b, j: (b, 0, j, 0)),
            seq_spec, seq_spec,
            pl.BlockSpec((1, ch), lambda b, j: (0, 0)),
        ],
        out_specs=seq_spec,
        out_shape=jax.ShapeDtypeStruct((bsz, kz, n2, ch), F32),
        compiler_params=_params(("parallel", "parallel")),
        name="dft_stage_c",
    )(gmat, w, x0, z, skip_d)


def _dft_tables(n1, kz, kzp):
    assert n1 % 2 == 0
    n2 = FFT_N2
    n = n1 * n2
    f1 = np.arange(_round_up(n1 // 2 + 1, SUBLANES))
    s1 = np.arange(kzp)
    ang = (2.0 * np.pi / n1) * ((f1[:, None] * s1[None, :]) % n1)
    live = (s1 < kz)[None, :]
    fa = np.concatenate([np.where(live, np.cos(ang), 0.0), np.where(live, -np.sin(ang), 0.0)], axis=0)
    weight = np.where((f1 == 0) | (f1 == n1 // 2), 1.0, np.where(f1 < n1 // 2, 2.0, 0.0))
    ga = np.concatenate([np.cos(ang).T * weight, -np.sin(ang).T * weight], axis=1)
    ga = ga * np.where(s1 < kz, 1.0 / n, 0.0)[:, None]
    idx = np.arange(n2)
    ang2 = (2.0 * np.pi / n2) * ((idx[:, None] * idx[None, :]) % n2)
    cr, ci = np.cos(ang2), -np.sin(ang2)
    mc = np.block([[cr, -ci], [ci, cr]])
    mct = np.block([[cr, ci], [-ci, cr]])
    ang3 = (2.0 * np.pi / n) * ((f1[:, None] * idx[None, :]) % n)
    twr, twi = np.cos(ang3)[:, :, None], -np.sin(ang3)[:, :, None]

    def hi_lo(mat):
        hi = mat.astype(BF16)
        lo = (mat - hi.astype(np.float64)).astype(BF16)
        return jnp.asarray(np.stack([hi, lo]))

    return hi_lo(fa), jnp.asarray(ga, F32), hi_lo(mc), hi_lo(mct), jnp.asarray(twr, F32), jnp.asarray(twi, F32)


def _rms(x, g):
    return x * lax.rsqrt(jnp.mean(x * x, axis=-1, keepdims=True) + RMS_EPS) * g


def _mla_proj_kernel(l_true, tm,
                     x_ref, cos_ref, sin_ref, wqa_ref, qn_ref, wq1_ref, wq2_ref, wc_ref, kn_ref,
                     wk1_ref, wk2_ref, wkk_ref, wkv_ref, q_ref, k_ref, v_ref):
    i = pl.program_id(1)
    x = x_ref[0].astype(BF16)
    cos = cos_ref[...]
    sin = sin_ref[...]
    lane = lax.broadcasted_iota(jnp.int32, (1, HEAD_SLOT), 1)

    cq = _rms(jnp.dot(x, wqa_ref[...], preferred_element_type=F32), qn_ref[...]).astype(BF16)
    q1 = jnp.dot(cq, wq1_ref[...], preferred_element_type=F32)
    q2 = jnp.dot(cq, wq2_ref[...], preferred_element_type=F32)
    q_one = jnp.where(lane == PAD_LANE, 1.0, 0.0)

    ckv = _rms(jnp.dot(x, wc_ref[...], preferred_element_type=F32), kn_ref[...]).astype(BF16)
    kn = jnp.dot(ckv, wkk_ref[...], preferred_element_type=F32)
    vv = jnp.dot(ckv, wkv_ref[...], preferred_element_type=F32)
    k1 = jnp.dot(x, wk1_ref[...], preferred_element_type=F32)
    k2 = jnp.dot(x, wk2_ref[...], preferred_element_type=F32)
    rows = i * tm + lax.broadcasted_iota(jnp.int32, (tm, 1), 0)
    k_pe = k1 * cos + k2 * sin + jnp.where((rows >= l_true) & (lane == PAD_LANE), PAD_SCORE, 0.0)
    v_one = jnp.where(lane == SUM_LANE, 1.0, 0.0)

    for h in range(N_HEADS):
        sl = slice(h * HEAD_SLOT, (h + 1) * HEAD_SLOT)
        q_ref[0, :, sl] = (q1[:, sl] * cos + q2[:, sl] * sin + q_one).astype(BF16)
        k_ref[0, :, sl] = (kn[:, sl] + k_pe).astype(BF16)
        v_ref[0, :, sl] = (vv[:, sl] + v_one).astype(BF16)


def _mla_proj(h, l_true, tm, cos_t, sin_t, wqa, qn, wq1, wq2, wc, kn, wk1, wk2, wkk, wkv):
    bsz, lp, _ = h.shape
    wide = N_HEADS * HEAD_SLOT
    full = lambda a: pl.BlockSpec(a.shape, lambda b, i: (0,) * a.ndim)
    out_sds = jax.ShapeDtypeStruct((bsz, lp, wide), BF16)
    weights = (wqa, qn, wq1, wq2, wc, kn, wk1, wk2, wkk, wkv)
    return pl.pallas_call(
        functools.partial(_mla_proj_kernel, l_true, tm),
        grid=(bsz, lp // tm),
        in_specs=[
            pl.BlockSpec((1, tm, D_MODEL), lambda b, i: (b, i, 0)),
            pl.BlockSpec((tm, HEAD_SLOT), lambda b, i: (i, 0)),
            pl.BlockSpec((tm, HEAD_SLOT), lambda b, i: (i, 0)),
        ] + [full(w) for w in weights],
        out_specs=[pl.BlockSpec((1, tm, wide), lambda b, i: (b, i, 0))] * 3,
        out_shape=[out_sds, out_sds, out_sds],
        compiler_params=_params(("parallel", "parallel")),
        name="mla_proj",
    )(h, cos_t, sin_t, *weights)


def _attn_kernel(tk, tr, q_ref, k_ref, v_ref, o_ref, s0_ref, s1_ref, m_ref, acc_ref):
    tq = q_ref.shape[1]
    nk = k_ref.shape[1] // tk
    s_refs = (s0_ref, s1_ref)
    row_blocks = [slice(r * tr, (r + 1) * tr) for r in range(tq // tr)]

    def scores(c, slot):
        k = k_ref[0, pl.ds(pl.multiple_of(c * tk, tk), tk), :]
        for rows in row_blocks:
            s_refs[slot][rows, :] = lax.dot_general(q_ref[0, rows, :], k, (((1,), (1,)), ((), ())),
                                                    preferred_element_type=F32)

    def softmax_pv(c, slot):
        v = v_ref[0, pl.ds(pl.multiple_of(c * tk, tk), tk), :]
        for rows in row_blocks:
            s = s_refs[slot][rows, :]
            m_old = m_ref[rows, :]
            m_new = jnp.maximum(m_old, jnp.max(s, axis=-1, keepdims=True))
            p = jnp.exp2(s - m_new).astype(BF16)
            acc_ref[rows, :] = (jnp.exp2(m_old - m_new) * acc_ref[rows, :]
                                + jnp.dot(p, v, preferred_element_type=F32))
            m_ref[rows, :] = m_new

    def step(c, slot):
        scores(c + 1, 1 - slot)
        softmax_pv(c, slot)

    m_ref[...] = jnp.full_like(m_ref, -jnp.inf)
    acc_ref[...] = jnp.zeros_like(acc_ref)
    scores(0, 0)

    def pair(j, carry):
        step(2 * j, 0)
        step(2 * j + 1, 1)
        return carry

    lax.fori_loop(0, (nk - 1) // 2, pair, 0)
    if (nk - 1) % 2:
        step(nk - 2, (nk - 2) % 2)
    softmax_pv(nk - 1, (nk - 1) % 2)
    acc = acc_ref[...]
    o_ref[0] = (acc / acc[:, SUM_LANE:SUM_LANE + 1]).astype(o_ref.dtype)


def _attn_batch_kernel(tk, tr, q_ref, k_ref, v_ref, o_ref, s0_ref, s1_ref, m_ref, acc_ref):
    nb, tq, _ = q_ref.shape
    nk = k_ref.shape[1] // tk
    s_refs = (s0_ref, s1_ref)
    row_blocks = [slice(r * tr, (r + 1) * tr) for r in range(tq // tr)]

    def scores(b, c, slot):
        k = k_ref[b, c * tk:(c + 1) * tk, :]
        for rows in row_blocks:
            s_refs[slot][rows, :] = lax.dot_general(q_ref[b, rows, :], k, (((1,), (1,)), ((), ())),
                                                    preferred_element_type=F32)

    def softmax_pv(b, c, slot):
        v = v_ref[b, c * tk:(c + 1) * tk, :]
        for rows in row_blocks:
            s = s_refs[slot][rows, :]
            m_old = m_ref[rows, :]
            m_new = jnp.maximum(m_old, jnp.max(s, axis=-1, keepdims=True))
            p = jnp.exp2(s - m_new).astype(BF16)
            acc_ref[rows, :] = (jnp.exp2(m_old - m_new) * acc_ref[rows, :]
                                + jnp.dot(p, v, preferred_element_type=F32))
            m_ref[rows, :] = m_new

    scores(0, 0, 0)

    def one_batch(b, carry):
        m_ref[...] = jnp.full_like(m_ref, -jnp.inf)
        acc_ref[...] = jnp.zeros_like(acc_ref)
        for c in range(nk):
            if c + 1 < nk:
                scores(b, c + 1, (c + 1) % 2)
            else:
                scores(jnp.minimum(b + 1, nb - 1), 0, 0)
            softmax_pv(b, c, c % 2)
        acc = acc_ref[...]
        o_ref[b] = (acc / acc[:, SUM_LANE:SUM_LANE + 1]).astype(o_ref.dtype)
        return carry

    lax.fori_loop(0, nb, one_batch, 0)


def _attention(q, k, v, tq, tk, tr):
    bsz, lp, wide = q.shape
    scratch = [pltpu.VMEM((tq, tk), F32), pltpu.VMEM((tq, tk), F32),
               pltpu.VMEM((tq, 1), F32), pltpu.VMEM((tq, HEAD_SLOT), F32)]
    if bsz > 1 and (lp // tk) % 2 == 0:
        return pl.pallas_call(
            functools.partial(_attn_batch_kernel, tk, _divisor_tile(tq, 640, 2 * SUBLANES)),
            grid=(N_HEADS, lp // tq),
            in_specs=[
                pl.BlockSpec((bsz, tq, HEAD_SLOT), lambda h, i: (0, i, h)),
                pl.BlockSpec((bsz, lp, HEAD_SLOT), lambda h, i: (0, 0, h)),
                pl.BlockSpec((bsz, lp, HEAD_SLOT), lambda h, i: (0, 0, h)),
            ],
            out_specs=pl.BlockSpec((bsz, tq, HEAD_SLOT), lambda h, i: (0, i, h)),
            out_shape=jax.ShapeDtypeStruct((bsz, lp, wide), BF16),
            scratch_shapes=scratch,
            compiler_params=_params(("parallel", "arbitrary")),
            name="attention_batched",
        )(q, k, v)
    return pl.pallas_call(
        functools.partial(_attn_kernel, tk, tr),
        grid=(bsz, N_HEADS, lp // tq),
        in_specs=[
            pl.BlockSpec((1, tq, HEAD_SLOT), lambda b, h, i: (b, i, h)),
            pl.BlockSpec((1, lp, HEAD_SLOT), lambda b, h, i: (b, 0, h)),
            pl.BlockSpec((1, lp, HEAD_SLOT), lambda b, h, i: (b, 0, h)),
        ],
        out_specs=pl.BlockSpec((1, tq, HEAD_SLOT), lambda b, h, i: (b, i, h)),
        out_shape=jax.ShapeDtypeStruct((bsz, lp, wide), BF16),
        scratch_shapes=scratch,
        compiler_params=_params(("parallel", "parallel", "arbitrary")),
        name="attention",
    )(q, k, v)


def _slot_cols(w, per_head, pieces):
    k = w.shape[0]
    wh = w.reshape(k, N_HEADS, per_head)
    out = jnp.zeros((k, N_HEADS, HEAD_SLOT), w.dtype)
    for dst, src, width, sign in pieces:
        out = out.at[:, :, dst:dst + width].set(sign * wh[:, :, src:src + width])
    return out.reshape(k, N_HEADS * HEAD_SLOT)


def _mla_weights(wq_b, wkv_a, wkv_b, wo):
    half = QK_ROPE // 2
    qk = QK_NOPE + QK_ROPE
    q_scale = (qk ** -0.5) * math.log2(math.e)
    wq1 = _slot_cols(wq_b, qk, [(0, 0, QK_NOPE, 1.0), (QK_NOPE, QK_NOPE, QK_ROPE, 1.0)]) * q_scale
    wq2 = _slot_cols(wq_b, qk, [(QK_NOPE, QK_NOPE + half, half, -1.0),
                                (QK_NOPE + half, QK_NOPE, half, 1.0)]) * q_scale
    wc = wkv_a[:, :KV_RANK]
    pe = wkv_a[:, KV_RANK:]
    zeros = lambda n: jnp.zeros((D_MODEL, n), wkv_a.dtype)
    tail = HEAD_SLOT - QK_NOPE - QK_ROPE
    wk1 = jnp.concatenate([zeros(QK_NOPE), pe, zeros(tail)], axis=1)
    wk2 = jnp.concatenate([zeros(QK_NOPE), -pe[:, half:], pe[:, :half], zeros(tail)], axis=1)
    wkk = _slot_cols(wkv_b, QK_NOPE + V_HEAD, [(0, 0, QK_NOPE, 1.0)])
    wkv = _slot_cols(wkv_b, QK_NOPE + V_HEAD, [(0, QK_NOPE, V_HEAD, 1.0)])
    wo_slot = jnp.zeros((N_HEADS, HEAD_SLOT, D_MODEL), wo.dtype)
    wo_slot = wo_slot.at[:, :V_HEAD, :].set(wo.reshape(N_HEADS, V_HEAD, D_MODEL))
    wo_slot = wo_slot.reshape(N_HEADS * HEAD_SLOT, D_MODEL)
    cast = lambda a: a.astype(BF16)
    return tuple(map(cast, (wq1, wq2, wc, wk1, wk2, wkk, wkv, wo_slot)))


def _rope_slot_tables(lp):
    pos = jnp.arange(lp, dtype=F32)
    inv = 1.0 / (ROPE_THETA ** (jnp.arange(0, QK_ROPE, 2, dtype=F32) / QK_ROPE))
    ang = pos[:, None] * inv[None, :]
    cos, sin = jnp.cos(ang), jnp.sin(ang)
    tail = HEAD_SLOT - QK_NOPE - QK_ROPE
    cos_t = jnp.concatenate([jnp.ones((lp, QK_NOPE), F32), cos, cos, jnp.zeros((lp, tail), F32)], axis=1)
    sin_t = jnp.concatenate([jnp.zeros((lp, QK_NOPE), F32), sin, sin, jnp.zeros((lp, tail), F32)], axis=1)
    return cos_t, sin_t


def _filter_features(l_true, lp):
    t = jnp.arange(lp, dtype=F32) / max(l_true - 1, 1)
    bands = (FILTER_EMB - 1) // 2
    freqs = jnp.linspace(1e-4, bands - 1, bands, dtype=F32)
    w = 2.0 * math.pi * jnp.arange(lp, dtype=F32) / l_true
    ang = w[:, None] * freqs[None, :]
    z = jnp.concatenate([t[:, None], jnp.cos(ang), -jnp.sin(ang)], axis=-1)
    return jnp.pad(z, ((0, 0), (0, FILTER_EMB_PAD - FILTER_EMB)))


def _divisor_tile(total, target, multiple):
    best = None
    for d in range(multiple, min(total, target) + 1, multiple):
        if total % d == 0:
            best = d
    assert best is not None, (total, target, multiple)
    return best


SHORT_SEQ_MAX = 2304
MXU_COLS = 256


def _tiling(bsz, l_true):
    lp = _round_up(l_true, LANES if l_true <= SHORT_SEQ_MAX else MXU_COLS)
    tk = _divisor_tile(lp, 1280, LANES // 2)
    tq = _divisor_tile(lp, 1280, 2 * SUBLANES)
    return dict(
        lp=lp, tk=tk,
        tl=_divisor_tile(lp, 640, HALO),
        tq=tq, tr=_divisor_tile(tq, 256, 2 * SUBLANES),
        tm=_divisor_tile(bsz * lp, 1088, 2 * SUBLANES),
        tmlp=_divisor_tile(bsz * lp, 640, 2 * SUBLANES),
    )


def _trunk(x, p):
    bsz, seq, _ = x.shape
    l_true = seq + N_META
    cfg = _tiling(bsz, l_true)
    lp = cfg["lp"]
    rows = bsz * lp

    meta = jnp.broadcast_to(p["meta_tokens"][None], (bsz, N_META, D_MODEL))
    h = jnp.concatenate([meta, x, jnp.zeros((bsz, lp - l_true, D_MODEL), F32)], axis=1)
    h = h.reshape(rows, D_MODEL)

    proj = _mm_bias(h, p["ev_w_in"], p["ev_b_in"], cfg["tm"]).reshape(bsz, lp, D_IN_EVEN)
    x0, z, y_b = _conv_mix(proj, l_true, cfg["tl"], p["ev_short_w"], p["ev_short_b"],
                           p["cf_dw_w"], p["cf_dw_b"], p["cf_ln_g"], p["cf_ln_b"])

    kz = lp // FFT_N2
    kzp = _round_up(kz, BF16_ROWS)
    n1 = _round_up(-(-(2 * l_true - 1) // FFT_N2), SUBLANES)
    fa, ga, mc, mct, twr, twi = _dft_tables(n1, kz, kzp)
    seq4 = lambda a: a.reshape(a.shape[0], kz, FFT_N2, D_HYENA)

    zf = _filter_features(l_true, lp)
    taps, sumsq = _hyena_filter(zf, l_true, cfg["tl"], p["hy_w1"], p["hy_b1"], p["hy_freq1"], p["hy_w2"],
                                p["hy_b2"], p["hy_freq2"], p["hy_w3"], p["hy_decay"])
    rs = lax.rsqrt(sumsq)
    kf = _dft_b_filter(mc, twr, twi, _dft_a(fa, seq4(taps)), rs)

    wb = _dft_b(mc, mct, twr, twi, _dft_a(fa, seq4(z)), kf)
    y_a = _dft_c(ga, wb, seq4(x0), seq4(z), p["hy_skip_d"]).reshape(rows, D_HYENA)

    h = _mm_res_ln([y_a, y_b.reshape(rows, D_CONF)], [p["ev_w_out_a"], p["ev_w_out_b"]], p["ev_b_out"],
                   h, p["ln1_g"][0], p["ln1_b"][0], cfg["tm"], "out_proj_even")
    h = _mlp(h, p["mlp_w1"][0], p["mlp_w2"][0], p["ln2_g"][0], p["ln2_b"][0], cfg["tmlp"])

    cos_t, sin_t = _rope_slot_tables(lp)
    q, k, v = _mla_proj(h.reshape(bsz, lp, D_MODEL), l_true, cfg["tl"], cos_t, sin_t,
                        p["mla_wq_a"], p["mla_q_norm"], p["wq1"], p["wq2"], p["wc"], p["mla_kv_norm"],
                        p["wk1"], p["wk2"], p["wkk"], p["wkv"])
    o = _attention(q, k, v, cfg["tq"], cfg["tk"], cfg["tr"]).reshape(rows, N_HEADS * HEAD_SLOT)
    h = _mm_res_ln([o], [p["wo_slot"]], jnp.zeros((1, D_MODEL), F32), h, p["ln1_g"][1], p["ln1_b"][1],
                   cfg["tm"], "out_proj_odd")
    return _mlp_final(h.reshape(bsz, lp, D_MODEL), seq, p["mlp_w1"][1], p["mlp_w2"][1], p["ln2_g"][1],
                      p["ln2_b"][1], _divisor_tile(seq, 640, N_META))


def kernel(x_prompt, x_sample, meta_tokens, ev_w_in, ev_b_in, ev_short_w, ev_short_b, hy_w1, hy_b1, hy_freq1,
           hy_w2, hy_b2, hy_freq2, hy_w3, hy_decay, hy_skip_d, cf_dw_w, cf_dw_b, cf_ln_g, cf_ln_b, ev_w_out,
           ev_b_out, mla_wq_a, mla_q_norm, mla_wq_b, mla_wkv_a, mla_kv_norm, mla_wkv_b, mla_wo, ln1_g, ln1_b,
           mlp_w1, mlp_w2, ln2_g, ln2_b):
    row = lambda a: a.reshape(1, -1)
    wq1, wq2, wc, wk1, wk2, wkk, wkv, wo_slot = _mla_weights(mla_wq_b[0], mla_wkv_a[0], mla_wkv_b[0], mla_wo[0])
    p = dict(
        meta_tokens=meta_tokens,
        ev_w_in=ev_w_in[0].astype(BF16), ev_b_in=row(ev_b_in[0]),
        ev_short_w=ev_short_w[0], ev_short_b=row(ev_short_b[0]),
        hy_w1=jnp.pad(hy_w1[0], ((0, 0), (0, FILTER_EMB_PAD - FILTER_EMB), (0, 0))),
        hy_b1=hy_b1[0][:, None, :], hy_freq1=hy_freq1[0][:, None, :],
        hy_w2=hy_w2[0], hy_b2=hy_b2[0][:, None, :], hy_freq2=hy_freq2[0][:, None, :],
        hy_w3=hy_w3[0], hy_decay=hy_decay[0][:, None, :], hy_skip_d=row(hy_skip_d[0]),
        cf_dw_w=cf_dw_w[0], cf_dw_b=row(cf_dw_b[0]), cf_ln_g=row(cf_ln_g[0]), cf_ln_b=row(cf_ln_b[0]),
        ev_w_out_a=ev_w_out[0, :D_HYENA].astype(BF16), ev_w_out_b=ev_w_out[0, D_HYENA:].astype(BF16),
        ev_b_out=row(ev_b_out[0]),
        mla_wq_a=mla_wq_a[0].astype(BF16), mla_q_norm=row(mla_q_norm[0]), mla_kv_norm=row(mla_kv_norm[0]),
        wq1=wq1, wq2=wq2, wc=wc, wk1=wk1, wk2=wk2, wkk=wkk, wkv=wkv, wo_slot=wo_slot,
        ln1_g=ln1_g[:, None, :], ln1_b=ln1_b[:, None, :], ln2_g=ln2_g[:, None, :], ln2_b=ln2_b[:, None, :],
        mlp_w1=mlp_w1.astype(BF16), mlp_w2=mlp_w2.astype(BF16),
    )
    return (_trunk(x_prompt, p), _trunk(x_sample, p))
```
